```python
import jax, jax.numpy as jnp
from jax import lax
import numpy as np

D_MODEL = 1024
BATCH = 8
SEQ = 2048
DEPTH = 4
DEC_BATCH = 32
DEC_SEQ = 64
PAST_LEN = 2048

CHUNK = 64
N_EVEN = (DEPTH + 1) // 2
N_ODD = DEPTH // 2

A_WIDTH = D_MODEL // 2
A_GROUPS = 4
A_GROUP_DIM = A_WIDTH // A_GROUPS
A_CHUNK = 128
B_WIDTH = D_MODEL // 2
B_BLOCKS = 8
B_BLOCK_DIM = B_WIDTH // B_BLOCKS
B_CONV = 4
B_C = 8.0
AB_IN = 2 * A_WIDTH + 2 * B_WIDTH
C_HEADS = 16
C_HEAD_DIM = D_MODEL // C_HEADS
C_QBLOCK = 128
C_SCALE = C_HEAD_DIM ** -0.5
N_MEM = 256
M_HEADS = 4
M_HEAD_DIM = 128
M_WIDTH = M_HEADS * M_HEAD_DIM
M_SCALE = M_HEAD_DIM ** -0.5
N_EXPERTS = 32
TOP_K = 4
D_EXPERT = D_MODEL
SWIGLU_LIMIT = 7.0
SWIGLU_ALPHA = 1.702
MOE_BLOCK = 128
DN_ALPHA = (2 * DEPTH) ** 0.25
DN_BETA = (8 * DEPTH) ** -0.25
LN_EPS = 1e-5

kernel_name = 'hybrid_stream_gmlp_rglru_fox_moe_step'


def layer_norm(x, g, b):
    xf = x.astype(jnp.float32)
    mu = jnp.mean(xf, axis=-1, keepdims=True)
    var = jnp.mean(jnp.square(xf - mu), axis=-1, keepdims=True)
    return ((xf - mu) * lax.rsqrt(var + LN_EPS)).astype(x.dtype) * g + b


def chunk_spatial_gate(u, v, ln_g, ln_b, w_s, b_s):
    Bn, L, _ = v.shape
    vn = layer_norm(v.reshape(Bn, L, A_GROUPS, A_GROUP_DIM), ln_g, ln_b)
    n_c = -(-L // A_CHUNK)
    vp = jnp.pad(vn, ((0, 0), (0, n_c * A_CHUNK - L), (0, 0), (0, 0)))
    vp = vp.reshape(Bn, n_c, A_CHUNK, A_GROUPS, A_GROUP_DIM)
    w_causal = jnp.where(jnp.tril(jnp.ones((A_CHUNK, A_CHUNK), dtype=bool)), w_s, 0)
    z = jnp.einsum('gts,bcsgd->bctgd', w_causal, vp) + b_s.T[:, :, None]
    z = z.reshape(Bn, n_c * A_CHUNK, A_WIDTH)[:, :L]
    return u * z, vn.reshape(Bn, L, A_WIDTH)


def causal_conv(x, buf, w, b):
    L = x.shape[1]
    xp = jnp.concatenate([buf.astype(x.dtype), x], axis=1)
    y = b + sum(xp[:, k:k + L] * w[k] for k in range(B_CONV))
    return y, xp[:, L:]


def _lin_combine(left, right):
    a_l, b_l = left
    a_r, b_r = right
    return a_l * a_r, a_r * b_l + b_r


def rg_lru(x, h0, w_a, b_a, w_x, b_x, lam):
    Bn, L, _ = x.shape
    xb = x.reshape(Bn, L, B_BLOCKS, B_BLOCK_DIM)
    r = jax.nn.sigmoid(jnp.einsum('blhi,hij->blhj', xb, w_a).reshape(Bn, L, B_WIDTH) + b_a)
    i = jax.nn.sigmoid(jnp.einsum('blhi,hij->blhj', xb, w_x).reshape(Bn, L, B_WIDTH) + b_x)
    log_a = -B_C * r.astype(jnp.float32) * jax.nn.softplus(-lam.astype(jnp.float32))
    a = jnp.exp(log_a)
    b = (x * i).astype(jnp.float32) * jnp.sqrt(-jnp.expm1(2.0 * log_a))
    b = b.at[:, 0].add(a[:, 0] * h0.astype(jnp.float32))
    _, h = lax.associative_scan(_lin_combine, (a, b), axis=1)
    return h


def ab_mixer(x, conv_buf, h0, w_in, a_ln_g, a_ln_b, a_w_s, a_b_s, conv_w, conv_b,
             w_a, b_a, w_x, b_x, lam, w_out):
    proj = x @ w_in
    o1, o2, o3 = A_WIDTH, 2 * A_WIDTH, 2 * A_WIDTH + B_WIDTH
    u = jax.nn.gelu(proj[..., :o1])
    v = jax.nn.gelu(proj[..., o1:o2])
    y_br = proj[..., o2:o3]
    x_br = proj[..., o3:]
    a_out, v_rows = chunk_spatial_gate(u, v, a_ln_g, a_ln_b, a_w_s, a_b_s)
    xc, new_buf = causal_conv(x_br, conv_buf, conv_w, conv_b)
    h = rg_lru(xc, h0, w_a, b_a, w_x, b_x, lam)
    b_out = jax.nn.gelu(y_br) * h.astype(x.dtype)
    out = jnp.concatenate([a_out, b_out], axis=-1) @ w_out
    return out, v_rows, new_buf, h[:, -1]


def fox_mixer(x, past, w_in, b_f, w_out):
    Bn, L, _ = x.shape
    proj = x @ w_in
    q = proj[..., :D_MODEL].reshape(Bn, L, C_HEADS, C_HEAD_DIM)
    k = proj[..., D_MODEL:2 * D_MODEL].reshape(Bn, L, C_HEADS, C_HEAD_DIM)
    v = proj[..., 2 * D_MODEL:3 * D_MODEL].reshape(Bn, L, C_HEADS, C_HEAD_DIM)
    logf = jax.nn.log_sigmoid(proj[..., 3 * D_MODEL:].astype(jnp.float32) + b_f)
    if past is None:
        k_all, v_all, lf_all, P = k, v, logf, 0
    else:
        k_p, v_p, lf_p = past
        k_all = jnp.concatenate([k_p.astype(k.dtype), k], axis=1)
        v_all = jnp.concatenate([v_p.astype(v.dtype), v], axis=1)
        lf_all = jnp.concatenate([lf_p.astype(jnp.float32), logf], axis=1)
        P = k_p.shape[1]
    cum_t = jnp.cumsum(lf_all, axis=1).transpose(0, 2, 1)
    cum_q = cum_t[:, :, P:]
    qb = min(C_QBLOCK, L)
    nb = L // qb
    q_blk = q.reshape(Bn, nb, qb, C_HEADS, C_HEAD_DIM).transpose(1, 0, 2, 3, 4)
    cq_blk = cum_q.reshape(Bn, C_HEADS, nb, qb).transpose(2, 0, 1, 3)
    key_pos = jnp.arange(P + L)

    def attend(args):
        q_i, cq_i, q_start = args
        s = jnp.einsum('bqhd,bkhd->bhqk', q_i, k_all).astype(jnp.float32) * C_SCALE
        s = s + cq_i[..., None] - cum_t[:, :, None, :]
        q_pos = P + q_start + jnp.arange(qb)
        s = jnp.where(key_pos[None, :] <= q_pos[:, None], s, -jnp.inf)
        p = jax.nn.softmax(s, axis=-1).astype(v_all.dtype)
        return jnp.einsum('bhqk,bkhd->bqhd', p, v_all)

    o = lax.map(attend, (q_blk, cq_blk, jnp.arange(nb) * qb))
    o = o.transpose(1, 0, 2, 3, 4).reshape(Bn, L, D_MODEL)
    return o @ w_out, k, v, logf


def mem_attend(x, mk, mv, w_q, w_o):
    Bn, L, _ = x.shape
    q = (x @ w_q).reshape(Bn, L, M_HEADS, M_HEAD_DIM)
    s = jnp.einsum('blhd,bmhd->bhlm', q, mk).astype(jnp.float32) * M_SCALE
    p = jax.nn.softmax(s, axis=-1).astype(mv.dtype)
    o = jnp.einsum('bhlm,bmhd->blhd', p, mv).reshape(Bn, L, M_WIDTH)
    return o @ w_o


def moe(x, w_r, b_r, w_gu, b_gu, w_dn, b_dn):
    Bn, L, D = x.shape
    xt = x.reshape(Bn * L, D)
    T = xt.shape[0]
    logits = (xt @ w_r).astype(jnp.float32) + b_r
    top_v, top_i = lax.top_k(logits, TOP_K)
    top_w = jax.nn.softmax(top_v, axis=-1)
    gates = jnp.einsum('tk,tke->te', top_w, jax.nn.one_hot(top_i, N_EXPERTS, dtype=jnp.float32)).astype(x.dtype)
    n_blk = -(-T // MOE_BLOCK)
    pad = n_blk * MOE_BLOCK - T
    x_b = jnp.pad(xt, ((0, pad), (0, 0))).reshape(n_blk, MOE_BLOCK, D)
    g_b = jnp.pad(gates, ((0, pad), (0, 0))).reshape(n_blk, MOE_BLOCK, N_EXPERTS)

    def block(args):
        xb, gb = args
        gu = jnp.einsum('td,edf->tef', xb, w_gu) + b_gu
        g = jnp.minimum(gu[..., :D_EXPERT], SWIGLU_LIMIT)
        u = jnp.clip(gu[..., D_EXPERT:], -SWIGLU_LIMIT, SWIGLU_LIMIT)
        h = (u + 1.0) * g * jax.nn.sigmoid(SWIGLU_ALPHA * g) * gb[..., None]
        return jnp.einsum('tef,efd->td', h, w_dn) + gb @ b_dn

    y = lax.map(block, (x_b, g_b)).reshape(n_blk * MOE_BLOCK, D)[:T]
    return y.reshape(Bn, L, D)


def trunk(x, W, mem_k, mem_v, conv0, h0s, ck0, cv0, clf0):
    Bn = x.shape[0]
    st = {'a_v': [], 'b_conv': [], 'b_h': [], 'c_k': [], 'c_v': [], 'c_logf': []}
    for layer in range(DEPTH):
        j = layer // 2
        if layer % 2 == 0:
            buf = jnp.zeros((Bn, B_CONV - 1, B_WIDTH), x.dtype) if conv0 is None else conv0[j]
            h0 = jnp.zeros((Bn, B_WIDTH), jnp.float32) if h0s is None else h0s[j]
            mix, v_rows, buf_new, h_last = ab_mixer(
                x, buf, h0, W['w_in_ab'][j], W['a_ln_g'][j], W['a_ln_b'][j], W['a_w_s'][j], W['a_b_s'][j],
                W['b_conv_w'][j], W['b_conv_b'][j], W['b_w_a'][j], W['b_b_a'][j], W['b_w_x'][j], W['b_b_x'][j],
                W['b_lambda'][j], W['w_out_ab'][j])
            st['a_v'].append(v_rows)
            st['b_conv'].append(buf_new)
            st['b_h'].append(h_last)
        else:
            past = None if ck0 is None else (ck0[j], cv0[j], clf0[j])
            mix, k, v, lf = fox_mixer(x, past, W['c_w_in'][j], W['c_b_f'][j], W['c_w_out'][j])
            st['c_k'].append(k)
            st['c_v'].append(v)
            st['c_logf'].append(lf)
        x = layer_norm(DN_ALPHA * x + mix, W['ln1_g'][layer], W['ln1_b'][layer])
        x = layer_norm(DN_ALPHA * x + mem_attend(x, mem_k[layer], mem_v[layer], W['w_mq'][layer], W['w_mo'][layer]),
                       W['ln2_g'][layer], W['ln2_b'][layer])
        x = layer_norm(DN_ALPHA * x + moe(x, W['w_router'][layer], W['b_router'][layer], W['w_gu'][layer],
                                          W['b_gu'][layer], W['w_dn'][layer], W['b_dn'][layer]),
                       W['ln3_g'][layer], W['ln3_b'][layer])
    return x, st


def setup_inputs(seed: int = 0) -> dict:
    key = jax.random.key(seed)
    keys = iter(jax.random.split(key, 64))

    def nrm(shape, scale=1.0):
        return scale * jax.random.normal(next(keys), shape, jnp.float32)

    def unif(shape, lo, hi):
        return jax.random.uniform(next(keys), shape, jnp.float32, lo, hi)

    d_in = D_MODEL ** -0.5
    inp = {}
    inp['x_prompt'] = nrm((BATCH, SEQ, D_MODEL))
    inp['x_sample'] = nrm((DEC_BATCH, DEC_SEQ, D_MODEL))
    inp['state_b_conv'] = nrm((N_EVEN, DEC_BATCH, B_CONV - 1, B_WIDTH))
    inp['state_b_h'] = nrm((N_EVEN, DEC_BATCH, B_WIDTH), 0.5)
    inp['cache_c_k'] = nrm((N_ODD, DEC_BATCH, PAST_LEN, C_HEADS, C_HEAD_DIM))
    inp['cache_c_v'] = nrm((N_ODD, DEC_BATCH, PAST_LEN, C_HEADS, C_HEAD_DIM))
    inp['cache_c_logf'] = jax.nn.log_sigmoid(2.5 + nrm((N_ODD, DEC_BATCH, PAST_LEN, C_HEADS)))
    inp['cache_mem_k'] = nrm((DEPTH, DEC_BATCH, N_MEM, M_HEADS, M_HEAD_DIM))
    inp['cache_mem_v'] = nrm((DEPTH, DEC_BATCH, N_MEM, M_HEADS, M_HEAD_DIM))
    inp['mem_prompt'] = nrm((BATCH, N_MEM, D_MODEL))
    inp['w_in_ab'] = nrm((N_EVEN, D_MODEL, AB_IN), d_in)
    inp['a_ln_g'] = 1.0 + nrm((N_EVEN, A_GROUPS, A_GROUP_DIM), 0.02)
    inp['a_ln_b'] = nrm((N_EVEN, A_GROUPS, A_GROUP_DIM), 0.02)
    inp['a_w_s'] = nrm((N_EVEN, A_GROUPS, A_CHUNK, A_CHUNK), 0.5 * A_CHUNK ** -0.5)
    inp['a_b_s'] = 1.0 + nrm((N_EVEN, A_GROUPS, A_CHUNK), 0.1)
    inp['b_conv_w'] = nrm((N_EVEN, B_CONV, B_WIDTH), B_CONV ** -0.5)
    inp['b_conv_b'] = nrm((N_EVEN, B_WIDTH), 0.02)
    inp['b_w_a'] = nrm((N_EVEN, B_BLOCKS, B_BLOCK_DIM, B_BLOCK_DIM), B_BLOCK_DIM ** -0.5)
    inp['b_b_a'] = nrm((N_EVEN, B_WIDTH), 0.02)
    inp['b_w_x'] = nrm((N_EVEN, B_BLOCKS, B_BLOCK_DIM, B_BLOCK_DIM), B_BLOCK_DIM ** -0.5)
    inp['b_b_x'] = nrm((N_EVEN, B_WIDTH), 0.02)
    a_c = unif((N_EVEN, B_WIDTH), 0.9, 0.999) ** (1.0 / B_C)
    inp['b_lambda'] = jnp.log(a_c) - jnp.log1p(-a_c)
    inp['w_out_ab'] = nrm((N_EVEN, A_WIDTH + B_WIDTH, D_MODEL), DN_BETA * (A_WIDTH + B_WIDTH) ** -0.5)
    inp['c_w_in'] = nrm((N_ODD, D_MODEL, 3 * D_MODEL + C_HEADS), d_in)
    inp['c_b_f'] = unif((N_ODD, C_HEADS), 1.0, 4.0)
    inp['c_w_out'] = nrm((N_ODD, D_MODEL, D_MODEL), DN_BETA * d_in)
    inp['w_mq'] = nrm((DEPTH, D_MODEL, M_WIDTH), d_in)
    inp['w_mk'] = nrm((DEPTH, D_MODEL, M_WIDTH), d_in)
    inp['w_mv'] = nrm((DEPTH, D_MODEL, M_WIDTH), d_in)
    inp['w_mo'] = nrm((DEPTH, M_WIDTH, D_MODEL), DN_BETA * M_WIDTH ** -0.5)
    inp['w_router'] = nrm((DEPTH, D_MODEL, N_EXPERTS), d_in)
    inp['b_router'] = nrm((DEPTH, N_EXPERTS), 0.01)
    inp['w_gu'] = nrm((DEPTH, N_EXPERTS, D_MODEL, 2 * D_EXPERT), d_in)
    inp['b_gu'] = nrm((DEPTH, N_EXPERTS, 2 * D_EXPERT), 0.01)
    inp['w_dn'] = nrm((DEPTH, N_EXPERTS, D_EXPERT, D_MODEL), DN_BETA * D_EXPERT ** -0.5)
    inp['b_dn'] = nrm((DEPTH, N_EXPERTS, D_MODEL), 0.01)
    inp['ln1_g'] = 1.0 + nrm((DEPTH, D_MODEL), 0.02)
    inp['ln1_b'] = nrm((DEPTH, D_MODEL), 0.02)
    inp['ln2_g'] = 1.0 + nrm((DEPTH, D_MODEL), 0.02)
    inp['ln2_b'] = nrm((DEPTH, D_MODEL), 0.02)
    inp['ln3_g'] = 1.0 + nrm((DEPTH, D_MODEL), 0.02)
    inp['ln3_b'] = nrm((DEPTH, D_MODEL), 0.02)
    return inp


def reference(x_prompt, x_sample, state_b_conv, state_b_h, cache_c_k, cache_c_v, cache_c_logf,
              cache_mem_k, cache_mem_v, mem_prompt,
              w_in_ab, a_ln_g, a_ln_b, a_w_s, a_b_s, b_conv_w, b_conv_b, b_w_a, b_b_a, b_w_x, b_b_x,
              b_lambda, w_out_ab, c_w_in, c_b_f, c_w_out, w_mq, w_mk, w_mv, w_mo,
              w_router, b_router, w_gu, b_gu, w_dn, b_dn,
              ln1_g, ln1_b, ln2_g, ln2_b, ln3_g, ln3_b):
    W = dict(w_in_ab=w_in_ab, a_ln_g=a_ln_g, a_ln_b=a_ln_b, a_w_s=a_w_s, a_b_s=a_b_s,
             b_conv_w=b_conv_w, b_conv_b=b_conv_b, b_w_a=b_w_a, b_b_a=b_b_a, b_w_x=b_w_x, b_b_x=b_b_x,
             b_lambda=b_lambda, w_out_ab=w_out_ab, c_w_in=c_w_in, c_b_f=c_b_f, c_w_out=c_w_out,
             w_mq=w_mq, w_mo=w_mo, w_router=w_router, b_router=b_router, w_gu=w_gu, b_gu=b_gu,
             w_dn=w_dn, b_dn=b_dn, ln1_g=ln1_g, ln1_b=ln1_b, ln2_g=ln2_g, ln2_b=ln2_b,
             ln3_g=ln3_g, ln3_b=ln3_b)
    Bp = mem_prompt.shape[0]
    p_mem_k = jnp.einsum('bmd,lde->lbme', mem_prompt, w_mk).reshape(DEPTH, Bp, N_MEM, M_HEADS, M_HEAD_DIM)
    p_mem_v = jnp.einsum('bmd,lde->lbme', mem_prompt, w_mv).reshape(DEPTH, Bp, N_MEM, M_HEADS, M_HEAD_DIM)
    y_prompt, sp = trunk(x_prompt, W, p_mem_k, p_mem_v, None, None, None, None, None)
    y_sample, ss = trunk(x_sample, W, cache_mem_k, cache_mem_v, state_b_conv, state_b_h,
                         cache_c_k, cache_c_v, cache_c_logf)
    p_b_conv = jnp.stack(sp['b_conv'])
    p_b_h = jnp.stack(sp['b_h'])
    p_c_k = jnp.stack(sp['c_k'])
    p_c_v = jnp.stack(sp['c_v'])
    p_c_logf = jnp.stack(sp['c_logf'])
    s_a_v = jnp.stack(ss['a_v'])
    s_b_conv = jnp.stack(ss['b_conv'])
    s_b_h = jnp.stack(ss['b_h'])
    s_c_k = jnp.stack(ss['c_k'])
    s_c_v = jnp.stack(ss['c_v'])
    s_c_logf = jnp.stack(ss['c_logf'])
    return (y_prompt, y_sample, p_b_conv, p_b_h, p_c_k, p_c_v, p_c_logf, p_mem_k, p_mem_v,
            s_a_v, s_b_conv, s_b_h, s_c_k, s_c_v, s_c_logf)
```

```python
import functools

import jax
import jax.numpy as jnp
from jax import lax
from jax.experimental import pallas as pl
from jax.experimental.pallas import tpu as pltpu

F32 = jnp.float32
BF16 = jnp.bfloat16
HIGHEST = lax.Precision.HIGHEST

D_MODEL = 1024
DEPTH = 4
A_WIDTH = 512
A_GROUPS = 4
A_GROUP_DIM = 128
A_CHUNK = 128
B_WIDTH = 512
B_BLOCKS = 8
B_BLOCK_DIM = 64
B_CONV = 4
B_C = 8.0
C_HEADS = 16
C_HEAD_DIM = 64
C_SCALE = C_HEAD_DIM ** -0.5
N_MEM = 256
M_HEADS = 4
M_HEAD_DIM = 128
M_WIDTH = 512
M_SCALE = M_HEAD_DIM ** -0.5
N_EXPERTS = 32
TOP_K = 4
D_EXPERT = 1024
SWIGLU_LIMIT = 7.0
SWIGLU_ALPHA = 1.702
DN_ALPHA = (2 * DEPTH) ** 0.25
LN_EPS = 1e-5

LANES = 128
SUBLANES = 8
VMEM_LIMIT = 56 * 1024 * 1024
NEG_INF = float("-inf")

MOE_TM = 256
ROW_TM = 512
ATT_T = 256
DISP_TM = 256


def _cparams(sem):
    return pltpu.CompilerParams(dimension_semantics=sem, vmem_limit_bytes=VMEM_LIMIT)


def _dot(a, b):
    return jnp.dot(a, b, preferred_element_type=F32)


def _dot_nt(a, b):
    return lax.dot_general(a, b, (((1,), (1,)), ((), ())), preferred_element_type=F32)


def _gelu(x):
    return 0.5 * x * (1.0 + jnp.tanh(0.7978845608028654 * (x + 0.044715 * (x * x * x))))


def _sigmoid(x):
    return 1.0 / (1.0 + jnp.exp(-x))


def _layer_norm(y, g, b):
    mu = jnp.mean(y, axis=-1, keepdims=True)
    d = y - mu
    var = jnp.mean(d * d, axis=-1, keepdims=True)
    return d * lax.rsqrt(var + LN_EPS) * g + b


def _full(shape):
    n = len(shape)
    return pl.BlockSpec(shape, lambda *_: (0,) * n)


def _proj_kernel(x_ref, w_ref, *o_refs):
    xb = x_ref[...].astype(BF16)
    n = o_refs[0].shape[-1]
    for j, o_ref in enumerate(o_refs):
        o_ref[...] = _dot(xb, w_ref[:, j * n:(j + 1) * n])


def _proj(x, w_bf16, n_out=1, tm=ROW_TM):
    T, K = x.shape
    N = w_bf16.shape[1]
    n = N // n_out
    tm = min(tm, T)
    outs = pl.pallas_call(
        _proj_kernel,
        grid=(T // tm,),
        in_specs=[pl.BlockSpec((tm, K), lambda i: (i, 0)), _full((K, N))],
        out_specs=[pl.BlockSpec((tm, n), lambda i: (i, 0)) for _ in range(n_out)],
        out_shape=[jax.ShapeDtypeStruct((T, n), F32) for _ in range(n_out)],
        compiler_params=_cparams(("parallel",)),
        name="proj",
    )(x, w_bf16)
    return outs


def _ab_kernel(x_ref, conv0_ref, h0_ref, w_in_ref, lng_ref, lnb_ref, ws_ref, bst_ref,
               cw_ref, cb_ref, wa_ref, ba_ref, wx_ref, bx_ref, lam_ref, w_out_ref,
               g1_ref, b1_ref,
               x1_ref, v_ref, buf_ref, hl_ref,
               xp_s, h_s, *, tl, sub):
    l = pl.program_id(1)

    @pl.when(l == 0)
    def _():
        xp_s[0:SUBLANES, :] = conv0_ref[...]
        h_s[...] = h0_ref[...]

    x = x_ref[...]
    proj = _dot(x.astype(BF16), w_in_ref[...])
    u = _gelu(proj[:, 0:A_WIDTH])
    v = _gelu(proj[:, A_WIDTH:2 * A_WIDTH])
    y_br = proj[:, 2 * A_WIDTH:2 * A_WIDTH + B_WIDTH]
    x_br = proj[:, 2 * A_WIDTH + B_WIDTH:]

    row = lax.broadcasted_iota(jnp.int32, (sub, sub), 0)
    col = lax.broadcasted_iota(jnp.int32, (sub, sub), 1)
    causal = col <= row
    out = DN_ALPHA * x
    for g in range(A_GROUPS):
        lo, hi = g * A_GROUP_DIM, (g + 1) * A_GROUP_DIM
        vn = _layer_norm(v[:, lo:hi], lng_ref[:, lo:hi], lnb_ref[:, lo:hi])
        v_ref[:, lo:hi] = vn
        wg = jnp.where(causal, ws_ref[g, 0:sub, 0:sub], 0.0).astype(BF16)
        bias = bst_ref[0:sub, g:g + 1]
        zs = []
        for c in range(tl // sub):
            vc = vn[c * sub:(c + 1) * sub, :].astype(BF16)
            zs.append(_dot(wg, vc) + bias)
        z = zs[0] if len(zs) == 1 else jnp.concatenate(zs, axis=0)
        a_out = u[:, lo:hi] * z
        out = out + _dot(a_out.astype(BF16), w_out_ref[lo:hi, :])

    xp_s[SUBLANES:SUBLANES + tl, :] = x_br
    xc = (cb_ref[...] + cw_ref[3:4, :] * x_br
          + cw_ref[2:3, :] * xp_s[SUBLANES - 1:SUBLANES - 1 + tl, :]
          + cw_ref[1:2, :] * xp_s[SUBLANES - 2:SUBLANES - 2 + tl, :]
          + cw_ref[0:1, :] * xp_s[SUBLANES - 3:SUBLANES - 3 + tl, :])
    tail = xp_s[tl:tl + SUBLANES, :]
    buf_ref[...] = tail
    xp_s[0:SUBLANES, :] = tail

    xcb = xc.astype(BF16)
    r = _sigmoid(_dot(xcb, wa_ref[...]) + ba_ref[...])
    i = _sigmoid(_dot(xcb, wx_ref[...]) + bx_ref[...])
    lam = lam_ref[...]
    softplus_neg_lam = jnp.maximum(-lam, 0.0) + jnp.log1p(jnp.exp(-jnp.abs(lam)))
    log_a = -B_C * r * softplus_neg_lam
    a = jnp.exp(log_a)
    one_minus_a2 = -jnp.tanh(log_a) * (a * a + 1.0)
    bb = (xc * i) * jnp.sqrt(one_minus_a2)

    rows = lax.broadcasted_iota(jnp.int32, (tl, B_WIDTH), 0)
    k = 1
    while k < tl:
        a_sh = pltpu.roll(a, k, 0)
        b_sh = pltpu.roll(bb, k, 0)
        m = rows >= k
        bb = jnp.where(m, a * b_sh + bb, bb)
        a = jnp.where(m, a * a_sh, a)
        k *= 2
    h = a * h_s[...] + bb
    h_last = h[tl - 1:tl, :]
    h_s[...] = h_last
    hl_ref[...] = h_last
    b_out = _gelu(y_br) * h
    out = out + _dot(b_out.astype(BF16), w_out_ref[A_WIDTH:, :])
    x1_ref[...] = _layer_norm(out, g1_ref[...], b1_ref[...])


def _ab_layer(x, conv0, h0, w):
    B, L, _ = x.shape
    tl = min(L, A_CHUNK)
    sub = min(tl, A_CHUNK)
    kern = functools.partial(_ab_kernel, tl=tl, sub=sub)
    row_spec = lambda n: pl.BlockSpec((None, tl, n), lambda b, l: (b, l, 0))
    st_spec = lambda r: pl.BlockSpec((None, r, B_WIDTH), lambda b, l: (b, 0, 0))
    weights = [w["w_in"], w["ln_g"], w["ln_b"], w["w_s"], w["b_st"], w["conv_w"], w["conv_b"],
               w["wa"], w["ba"], w["wx"], w["bx"], w["lam"], w["w_out"], w["g1"], w["b1"]]
    return pl.pallas_call(
        kern,
        grid=(B, L // tl),
        in_specs=[row_spec(D_MODEL), st_spec(SUBLANES), st_spec(1)] + [_full(a.shape) for a in weights],
        out_specs=[row_spec(D_MODEL), row_spec(A_WIDTH), st_spec(SUBLANES), st_spec(1)],
        out_shape=[jax.ShapeDtypeStruct((B, L, D_MODEL), F32),
                   jax.ShapeDtypeStruct((B, L, A_WIDTH), F32),
                   jax.ShapeDtypeStruct((B, SUBLANES, B_WIDTH), F32),
                   jax.ShapeDtypeStruct((B, 1, B_WIDTH), F32)],
        scratch_shapes=[pltpu.VMEM((tl + SUBLANES, B_WIDTH), F32), pltpu.VMEM((1, B_WIDTH), F32)],
        compiler_params=_cparams(("parallel", "arbitrary")),
        name="ab_mixer",
    )(x, conv0, h0, *weights)


def _fox_pre_kernel(x_ref, w_ref, wf_ref, bf_ref, q_ref, k_ref, v_ref, lf_ref):
    x = x_ref[...]
    xb = x.astype(BF16)
    q_ref[...] = _dot(xb, w_ref[:, 0:D_MODEL])
    k_ref[...] = _dot(xb, w_ref[:, D_MODEL:2 * D_MODEL])
    v_ref[...] = _dot(xb, w_ref[:, 2 * D_MODEL:3 * D_MODEL])
    f = jnp.dot(x, wf_ref[...], preferred_element_type=F32, precision=HIGHEST) + bf_ref[...]
    logf = jnp.minimum(f, 0.0) - jnp.log1p(jnp.exp(-jnp.abs(f)))
    lf_ref[...] = logf[:, 0:C_HEADS]


def _fox_pre(x, w_qkv, w_f, b_f, tm=ROW_TM):
    T = x.shape[0]
    tm = min(tm, T)
    row = lambda n: pl.BlockSpec((tm, n), lambda i: (i, 0))
    return pl.pallas_call(
        _fox_pre_kernel,
        grid=(T // tm,),
        in_specs=[row(D_MODEL), _full(w_qkv.shape), _full(w_f.shape), _full(b_f.shape)],
        out_specs=[row(D_MODEL), row(D_MODEL), row(D_MODEL), row(C_HEADS)],
        out_shape=[jax.ShapeDtypeStruct((T, D_MODEL), F32)] * 3 + [jax.ShapeDtypeStruct((T, C_HEADS), F32)],
        compiler_params=_cparams(("parallel",)),
        name="fox_pre",
    )(x, w_qkv, w_f, b_f)


def _cumsum_kernel(x_ref, o_ref):
    x = x_ref[...]
    n = x.shape[-1]
    lane = lax.broadcasted_iota(jnp.int32, x.shape, 1)
    k = 1
    while k < n:
        x = x + jnp.where(lane >= k, pltpu.roll(x, k, 1), 0.0)
        k *= 2
    o_ref[...] = x


def _cumsum_lanes(x):
    B, H, Lp = x.shape
    spec = pl.BlockSpec((None, H, Lp), lambda b: (b, 0, 0))
    return pl.pallas_call(
        _cumsum_kernel, grid=(B,), in_specs=[spec], out_specs=spec,
        out_shape=jax.ShapeDtypeStruct(x.shape, F32),
        compiler_params=_cparams(("parallel",)), name="logf_cumsum",
    )(x)


def _fox_attn_kernel(*refs, tq, tk, nq, n_past, has_past):
    if has_past:
        q_ref, k_ref, v_ref, kp_ref, vp_ref, cq_ref, ck_ref, o_ref, m_s, l_s, acc_s = refs
    else:
        q_ref, k_ref, v_ref, cq_ref, ck_ref, o_ref, m_s, l_s, acc_s = refs
        kp_ref = vp_ref = None
    qi = pl.program_id(2) if nq > 1 else 0
    past_len = n_past * ATT_T

    for hh in range(2):
        lo, hi = hh * C_HEAD_DIM, (hh + 1) * C_HEAD_DIM
        qh = (q_ref[:, lo:hi] * C_SCALE).astype(BF16)
        cq = cq_ref[hh]
        m_s[...] = jnp.full(m_s.shape, NEG_INF, F32)
        l_s[...] = jnp.zeros(l_s.shape, F32)
        acc_s[...] = jnp.zeros(acc_s.shape, F32)

        def step(kc, vc, ck, mask):
            s = _dot_nt(qh, kc.astype(BF16)) + (cq - ck)
            if mask is not None:
                s = jnp.where(mask, s, NEG_INF)
            m_old = m_s[...]
            m_new = jnp.maximum(m_old, jnp.max(s, axis=-1, keepdims=True))
            p = jnp.exp(s - m_new)
            alpha = jnp.exp(m_old - m_new)
            l_s[...] = alpha * l_s[...] + jnp.sum(p, axis=-1, keepdims=True)
            acc_s[...] = alpha * acc_s[...] + _dot(p.astype(BF16), vc.astype(BF16))
            m_s[...] = m_new

        if has_past:
            def past_body(j, c):
                off = pl.multiple_of(j * ATT_T, ATT_T)
                step(kp_ref[pl.ds(off, ATT_T), lo:hi], vp_ref[pl.ds(off, ATT_T), lo:hi],
                     ck_ref[hh:hh + 1, pl.ds(off, ATT_T)], None)
                return c
            lax.fori_loop(0, n_past, past_body, 0)

        def new_body(j, c):
            off = pl.multiple_of(j * tk, tk)
            coff = pl.multiple_of(past_len + j * tk, tk)
            step(k_ref[pl.ds(off, tk), lo:hi], v_ref[pl.ds(off, tk), lo:hi],
                 ck_ref[hh:hh + 1, pl.ds(coff, tk)], None)
            return c
        if nq > 1:
            lax.fori_loop(0, qi, new_body, 0)

        off = pl.multiple_of(qi * tk, tk) if nq > 1 else 0
        coff = pl.multiple_of(past_len + qi * tk, tk) if nq > 1 else past_len
        rr = lax.broadcasted_iota(jnp.int32, (tq, tk), 0)
        cc = lax.broadcasted_iota(jnp.int32, (tq, tk), 1)
        step(k_ref[pl.ds(off, tk), lo:hi], v_ref[pl.ds(off, tk), lo:hi],
             ck_ref[hh:hh + 1, pl.ds(coff, tk)], cc <= rr)
        o_ref[:, lo:hi] = acc_s[...] / l_s[...]


def _fox_attn(q, k, v, cq, ck, k_past=None, v_past=None):
    B, L, _ = q.shape
    has_past = k_past is not None
    P = k_past.shape[1] if has_past else 0
    tq = tk = min(L, ATT_T)
    nq = L // tq
    n_hp = C_HEADS // 2
    kern = functools.partial(_fox_attn_kernel, tq=tq, tk=tk, nq=nq, n_past=P // ATT_T, has_past=has_past)
    q_spec = pl.BlockSpec((None, tq, LANES), lambda b, h, i: (b, i, h))
    kv_spec = pl.BlockSpec((None, L, LANES), lambda b, h, i: (b, 0, h))
    in_specs = [q_spec, kv_spec, kv_spec]
    args = [q, k, v]
    if has_past:
        p_spec = pl.BlockSpec((None, P, LANES), lambda b, h, i: (b, 0, h))
        in_specs += [p_spec, p_spec]
        args += [k_past, v_past]
    in_specs += [pl.BlockSpec((None, None, 2, tq, 1), lambda b, h, i: (b, h, 0, i, 0)),
                 pl.BlockSpec((None, None, 2, ck.shape[-1]), lambda b, h, i: (b, h, 0, 0))]
    args += [cq, ck]
    return pl.pallas_call(
        kern,
        grid=(B, n_hp, nq),
        in_specs=in_specs,
        out_specs=q_spec,
        out_shape=jax.ShapeDtypeStruct((B, L, D_MODEL), F32),
        scratch_shapes=[pltpu.VMEM((tq, 1), F32), pltpu.VMEM((tq, 1), F32),
                        pltpu.VMEM((tq, C_HEAD_DIM), F32)],
        compiler_params=_cparams(("parallel", "parallel", "arbitrary")),
        name="fox_attn",
    )(*args)


def _out_ln_kernel(x_ref, o_ref, w_ref, g_ref, b_ref, y_ref):
    mix = _dot(o_ref[...].astype(BF16), w_ref[...])
    y_ref[...] = _layer_norm(DN_ALPHA * x_ref[...] + mix, g_ref[...], b_ref[...])


def _out_ln(x, o, w_bf16, g, b, tm=ROW_TM):
    T = x.shape[0]
    K = o.shape[1]
    tm = min(tm, T)
    return pl.pallas_call(
        _out_ln_kernel,
        grid=(T // tm,),
        in_specs=[pl.BlockSpec((tm, D_MODEL), lambda i: (i, 0)), pl.BlockSpec((tm, K), lambda i: (i, 0)),
                  _full(w_bf16.shape), _full(g.shape), _full(b.shape)],
        out_specs=pl.BlockSpec((tm, D_MODEL), lambda i: (i, 0)),
        out_shape=jax.ShapeDtypeStruct((T, D_MODEL), F32),
        compiler_params=_cparams(("parallel",)),
        name="out_ln",
    )(x, o, w_bf16, g, b)


def _fox_layer(x, past, w):
    B, L, _ = x.shape
    xt = x.reshape(B * L, D_MODEL)
    q, k, v, lf = _fox_pre(xt, w["w_qkv"], w["w_f"], w["b_f"])
    q, k, v = (a.reshape(B, L, D_MODEL) for a in (q, k, v))
    lf = lf.reshape(B, L, C_HEADS)
    lf_all = lf if past is None else jnp.concatenate([past[2], lf], axis=1)
    P = lf_all.shape[1] - L
    tot = lf_all.shape[1]
    tot_pad = -(-tot // LANES) * LANES
    lf_t = jnp.pad(lf_all.transpose(0, 2, 1), ((0, 0), (0, 0), (0, tot_pad - tot)))
    cum = _cumsum_lanes(lf_t)
    ck = cum.reshape(B, C_HEADS // 2, 2, tot_pad)
    cq = cum[:, :, P:P + L].reshape(B, C_HEADS // 2, 2, L, 1)
    if past is None:
        o = _fox_attn(q, k, v, cq, ck)
    else:
        o = _fox_attn(q, k, v, cq, ck, past[0], past[1])
    x1 = _out_ln(xt, o.reshape(B * L, D_MODEL), w["w_out"], w["g1"], w["b1"])
    return x1.reshape(B, L, D_MODEL), k, v, lf


def _mem_kernel(x_ref, mk_ref, mv_ref, wq_ref, wo_ref, g_ref, b_ref, y_ref):
    x = x_ref[...]
    q = _dot(x.astype(BF16), wq_ref[...])
    out = DN_ALPHA * x
    for h in range(M_HEADS):
        lo, hi = h * M_HEAD_DIM, (h + 1) * M_HEAD_DIM
        s = _dot_nt(q[:, lo:hi].astype(BF16), mk_ref[:, lo:hi].astype(BF16)) * M_SCALE
        s = s - jnp.max(s, axis=-1, keepdims=True)
        p = jnp.exp(s)
        p = p / jnp.sum(p, axis=-1, keepdims=True)
        o = _dot(p.astype(BF16), mv_ref[:, lo:hi].astype(BF16))
        out = out + _dot(o.astype(BF16), wo_ref[lo:hi, :])
    y_ref[...] = _layer_norm(out, g_ref[...], b_ref[...])


def _mem_layer(x, mk, mv, w_q, w_o, g, b, tl=256):
    B, L, _ = x.shape
    tl = min(tl, L)
    row = pl.BlockSpec((None, tl, D_MODEL), lambda bb, l: (bb, l, 0))
    mem = pl.BlockSpec((None, N_MEM, M_WIDTH), lambda bb, l: (bb, 0, 0))
    return pl.pallas_call(
        _mem_kernel,
        grid=(B, L // tl),
        in_specs=[row, mem, mem, _full(w_q.shape), _full(w_o.shape), _full(g.shape), _full(b.shape)],
        out_specs=row,
        out_shape=jax.ShapeDtypeStruct((B, L, D_MODEL), F32),
        compiler_params=_cparams(("parallel", "parallel")),
        name="mem_attend",
    )(x, mk, mv, w_q, w_o, g, b)


def _router_kernel(x_ref, wr_ref, br_ref, mi_ref, mw_ref, cnt_ref, carry_s, *, tm):
    @pl.when(pl.program_id(0) == 0)
    def _():
        carry_s[...] = jnp.zeros(carry_s.shape, F32)

    x = x_ref[...]
    logits = jnp.dot(x, wr_ref[...], preferred_element_type=F32, precision=HIGHEST) + br_ref[...]
    lane = lax.broadcasted_iota(jnp.int32, (tm, LANES), 1)
    work = jnp.where(lane < N_EXPERTS, logits, NEG_INF)
    vals, idxs = [], []
    for _ in range(TOP_K):
        mx = jnp.max(work, axis=-1, keepdims=True)
        idx = jnp.min(jnp.where(work == mx, lane, LANES), axis=-1, keepdims=True)
        vals.append(mx)
        idxs.append(idx)
        work = jnp.where(lane == idx, NEG_INF, work)
    es = [jnp.exp(vv - vals[0]) for vv in vals]
    den = es[0] + es[1] + es[2] + es[3]
    sel = jnp.zeros((tm, LANES), F32)
    for idx in idxs:
        sel = jnp.where(lane == idx, 1.0, sel)
    rr = lax.broadcasted_iota(jnp.int32, (tm, tm), 0)
    cc = lax.broadcasted_iota(jnp.int32, (tm, tm), 1)
    strict = jnp.where(cc < rr, 1.0, 0.0).astype(BF16)
    rank = carry_s[...] + _dot(strict, sel.astype(BF16))
    mi = jnp.zeros((tm, LANES), jnp.int32)
    mw = jnp.zeros((tm, LANES), F32)
    for kk in range(TOP_K):
        rk = jnp.sum(jnp.where(lane == idxs[kk], rank, 0.0), axis=-1, keepdims=True)
        mi = jnp.where(lane == kk, idxs[kk], mi)
        mi = jnp.where(lane == TOP_K + kk, rk.astype(jnp.int32), mi)
        mw = jnp.where(lane == kk, es[kk] / den, mw)
    mi_ref[...] = mi
    mw_ref[...] = mw
    total = carry_s[...] + jnp.sum(sel, axis=0, keepdims=True)
    carry_s[...] = total
    cnt_ref[...] = total


def _router(x, w_r, b_r, tm=ROW_TM):
    T = x.shape[0]
    tm = min(tm, T)
    row = lambda: pl.BlockSpec((tm, LANES), lambda i: (i, 0))
    return pl.pallas_call(
        functools.partial(_router_kernel, tm=tm),
        grid=(T // tm,),
        in_specs=[pl.BlockSpec((tm, D_MODEL), lambda i: (i, 0)), _full(w_r.shape), _full(b_r.shape)],
        out_specs=[row(), row(), _full((1, LANES))],
        out_shape=[jax.ShapeDtypeStruct((T, LANES), jnp.int32), jax.ShapeDtypeStruct((T, LANES), F32),
                   jax.ShapeDtypeStruct((1, LANES), F32)],
        scratch_shapes=[pltpu.VMEM((1, LANES), F32)],
        compiler_params=_cparams(("arbitrary",)),
        name="router",
    )(x, w_r, b_r)


def _row_copy(src, src_row, dst, dst_row, sem):
    return pltpu.make_async_copy(src.at[pl.ds(src_row, 1)], dst.at[pl.ds(dst_row, 1)], sem)


def _dispatch_kernel(pos_ref, x_ref, xs_in_ref, xs_ref, sem, *, tm):
    del xs_in_ref
    base = pl.program_id(0) * tm * TOP_K

    def issue(t, c):
        for kk in range(TOP_K):
            _row_copy(x_ref, t, xs_ref, pos_ref[base + t * TOP_K + kk], sem).start()
        return c
    lax.fori_loop(0, tm, issue, 0)

    def drain(t, c):
        for kk in range(TOP_K):
            _row_copy(x_ref, 0, xs_ref, 0, sem).wait()
        return c
    lax.fori_loop(0, tm, drain, 0)


def _dispatch(pos_flat, x, xs_buf, tm=DISP_TM):
    T = x.shape[0]
    tm = min(tm, T)
    return pl.pallas_call(
        functools.partial(_dispatch_kernel, tm=tm),
        grid_spec=pltpu.PrefetchScalarGridSpec(
            num_scalar_prefetch=1,
            grid=(T // tm,),
            in_specs=[pl.BlockSpec((tm, D_MODEL), lambda i, pos: (i, 0)),
                      pl.BlockSpec(memory_space=pl.ANY)],
            out_specs=pl.BlockSpec(memory_space=pl.ANY),
            scratch_shapes=[pltpu.SemaphoreType.DMA(())],
        ),
        out_shape=jax.ShapeDtypeStruct(xs_buf.shape, F32),
        input_output_aliases={2: 0},
        compiler_params=_cparams(("arbitrary",)),
        name="moe_dispatch",
    )(pos_flat, x, xs_buf)


def _gmm_kernel(te_ref, nu_ref, xs_ref, wgu_ref, bgu_ref, wdn_ref, bdn_ref, ys_ref):
    i = pl.program_id(0)

    @pl.when(i < nu_ref[0])
    def _():
        xb = xs_ref[...].astype(BF16)
        gu = _dot(xb, wgu_ref[...]) + bgu_ref[...]
        g = jnp.minimum(gu[:, 0:D_EXPERT], SWIGLU_LIMIT)
        u = jnp.clip(gu[:, D_EXPERT:], -SWIGLU_LIMIT, SWIGLU_LIMIT)
        h = (u + 1.0) * g * _sigmoid(SWIGLU_ALPHA * g)
        ys_ref[...] = _dot(h.astype(BF16), wdn_ref[...]) + bdn_ref[...]

    @pl.when(i >= nu_ref[0])
    def _():
        ys_ref[...] = jnp.zeros(ys_ref.shape, F32)


def _gmm(tile_expert, n_used, xs, w_gu, b_gu, w_dn, b_dn, tm=MOE_TM):
    R = xs.shape[0]
    wmap = lambda i, te, nu: (te[i], 0, 0)
    return pl.pallas_call(
        _gmm_kernel,
        grid_spec=pltpu.PrefetchScalarGridSpec(
            num_scalar_prefetch=2,
            grid=(R // tm,),
            in_specs=[pl.BlockSpec((tm, D_MODEL), lambda i, te, nu: (i, 0)),
                      pl.BlockSpec((None, D_MODEL, 2 * D_EXPERT), wmap),
                      pl.BlockSpec((None, 1, 2 * D_EXPERT), wmap),
                      pl.BlockSpec((None, D_EXPERT, D_MODEL), wmap),
                      pl.BlockSpec((None, 1, D_MODEL), wmap)],
            out_specs=pl.BlockSpec((tm, D_MODEL), lambda i, te, nu: (i, 0)),
        ),
        out_shape=jax.ShapeDtypeStruct((R, D_MODEL), F32),
        compiler_params=_cparams(("arbitrary",)),
        name="moe_gmm",
    )(tile_expert, n_used, xs, w_gu, b_gu, w_dn, b_dn)


def _combine_kernel(pos_ref, x_ref, mw_ref, ys_ref, g_ref, b_ref, y_ref, buf, sem, *, tm):
    base = pl.program_id(0) * tm * TOP_K

    def issue(t, c):
        for kk in range(TOP_K):
            _row_copy(ys_ref, pos_ref[base + t * TOP_K + kk], buf.at[kk], t, sem).start()
        return c
    lax.fori_loop(0, tm, issue, 0)

    def drain(t, c):
        for kk in range(TOP_K):
            _row_copy(ys_ref, 0, buf.at[kk], 0, sem).wait()
        return c
    lax.fori_loop(0, tm, drain, 0)

    out = DN_ALPHA * x_ref[...]
    for kk in range(TOP_K):
        out = out + mw_ref[:, kk:kk + 1] * buf[kk]
    y_ref[...] = _layer_norm(out, g_ref[...], b_ref[...])


def _combine(pos_flat, x, meta_w, ys, g, b, tm=DISP_TM):
    T = x.shape[0]
    tm = min(tm, T)
    return pl.pallas_call(
        functools.partial(_combine_kernel, tm=tm),
        grid_spec=pltpu.PrefetchScalarGridSpec(
            num_scalar_prefetch=1,
            grid=(T // tm,),
            in_specs=[pl.BlockSpec((tm, D_MODEL), lambda i, pos: (i, 0)),
                      pl.BlockSpec((tm, LANES), lambda i, pos: (i, 0)),
                      pl.BlockSpec(memory_space=pl.ANY),
                      pl.BlockSpec((1, D_MODEL), lambda i, pos: (0, 0)),
                      pl.BlockSpec((1, D_MODEL), lambda i, pos: (0, 0))],
            out_specs=pl.BlockSpec((tm, D_MODEL), lambda i, pos: (i, 0)),
            scratch_shapes=[pltpu.VMEM((TOP_K, tm, D_MODEL), F32), pltpu.SemaphoreType.DMA(())],
        ),
        out_shape=jax.ShapeDtypeStruct((T, D_MODEL), F32),
        compiler_params=_cparams(("arbitrary",)),
        name="moe_combine",
    )(pos_flat, x, meta_w, ys, g, b)


def _moe_layer(x, xs_buf, w, tm=MOE_TM):
    T = x.shape[0]
    R = xs_buf.shape[0]
    n_tiles = R // tm
    meta_i, meta_w, counts = _router(x, w["w_r"], w["b_r"])
    cnt = counts[0, :N_EXPERTS].astype(jnp.int32)
    tiles_per = (cnt + tm - 1) // tm
    tile_end = jnp.cumsum(tiles_per)
    group_start = (tile_end - tiles_per) * tm
    n_used = tile_end[-1:]
    tile_ids = jnp.arange(n_tiles, dtype=jnp.int32)
    tile_expert = jnp.sum(tile_ids[:, None] >= tile_end[None, :], axis=1).astype(jnp.int32)
    last_expert = jnp.sum(jnp.maximum(n_used - 1, 0) >= tile_end).astype(jnp.int32)
    tile_expert = jnp.where(tile_ids < n_used, tile_expert, last_expert)
    pos = group_start[meta_i[:, :TOP_K]] + meta_i[:, TOP_K:2 * TOP_K]
    pos_flat = pos.reshape(T * TOP_K).astype(jnp.int32)
    xs = _dispatch(pos_flat, x, xs_buf)
    ys = _gmm(tile_expert, n_used.astype(jnp.int32), xs, w["w_gu"], w["b_gu"], w["w_dn"], w["b_dn"], tm)
    return _combine(pos_flat, x, meta_w, ys, w["g3"], w["b3"]), xs


def _block_diag(w):
    eye = jnp.eye(B_BLOCKS, dtype=w.dtype)
    return jnp.einsum("hij,hg->higj", w, eye).reshape(B_WIDTH, B_WIDTH)


def _pad_lanes(a, value=0.0):
    return jnp.pad(a, ((0, 0),) * (a.ndim - 1) + ((0, LANES - a.shape[-1]),), constant_values=value)


def kernel(x_prompt, x_sample, state_b_conv, state_b_h, cache_c_k, cache_c_v, cache_c_logf, cache_mem_k, cache_mem_v, mem_prompt, w_in_ab, a_ln_g, a_ln_b, a_w_s, a_b_s, b_conv_w, b_conv_b, b_w_a, b_b_a, b_w_x, b_b_x, b_lambda, w_out_ab, c_w_in, c_b_f, c_w_out, w_mq, w_mk, w_mv, w_mo, w_router, b_router, w_gu, b_gu, w_dn, b_dn, ln1_g, ln1_b, ln2_g, ln2_b, ln3_g, ln3_b):
    Bp, Lp, _ = x_prompt.shape
    Bs, Ls, _ = x_sample.shape
    Tp, Ts = Bp * Lp, Bs * Ls
    T = Tp + Ts
    row = lambda a: a.reshape(1, -1)

    mem_flat = mem_prompt.reshape(Bp * N_MEM, D_MODEL)
    w_mkv = jnp.concatenate([w_mk, w_mv], axis=-1).astype(BF16)
    p_mem = [_proj(mem_flat, w_mkv[l], n_out=2) for l in range(DEPTH)]
    p_mem_k = jnp.stack([p[0] for p in p_mem]).reshape(DEPTH, Bp, N_MEM, M_WIDTH)
    p_mem_v = jnp.stack([p[1] for p in p_mem]).reshape(DEPTH, Bp, N_MEM, M_WIDTH)
    s_mem_k = cache_mem_k.reshape(DEPTH, Bs, N_MEM, M_WIDTH)
    s_mem_v = cache_mem_v.reshape(DEPTH, Bs, N_MEM, M_WIDTH)

    R = T * TOP_K + N_EXPERTS * MOE_TM
    xs_buf = jnp.zeros((R, D_MODEL), F32)

    xp, xsm = x_prompt, x_sample
    st_p = {"b_conv": [], "b_h": [], "c_k": [], "c_v": [], "c_logf": []}
    st_s = {"a_v": [], "b_conv": [], "b_h": [], "c_k": [], "c_v": [], "c_logf": []}
    for layer in range(DEPTH):
        j = layer // 2
        g1, b1 = row(ln1_g[layer]), row(ln1_b[layer])
        if layer % 2 == 0:
            w = dict(w_in=w_in_ab[j].astype(BF16), ln_g=row(a_ln_g[j]), ln_b=row(a_ln_b[j]), w_s=a_w_s[j],
                     b_st=a_b_s[j].T, conv_w=b_conv_w[j], conv_b=row(b_conv_b[j]),
                     wa=_block_diag(b_w_a[j]).astype(BF16), ba=row(b_b_a[j]),
                     wx=_block_diag(b_w_x[j]).astype(BF16), bx=row(b_b_x[j]), lam=row(b_lambda[j]),
                     w_out=w_out_ab[j].astype(BF16), g1=g1, b1=b1)
            conv0_p = jnp.zeros((Bp, SUBLANES, B_WIDTH), F32)
            h0_p = jnp.zeros((Bp, 1, B_WIDTH), F32)
            xp, _, buf_p, hl_p = _ab_layer(xp, conv0_p, h0_p, w)
            conv0_s = jnp.pad(state_b_conv[j], ((0, 0), (SUBLANES - (B_CONV - 1), 0), (0, 0)))
            xsm, v_s, buf_s, hl_s = _ab_layer(xsm, conv0_s, state_b_h[j][:, None, :], w)
            st_p["b_conv"].append(buf_p[:, SUBLANES - (B_CONV - 1):])
            st_p["b_h"].append(hl_p[:, 0])
            st_s["a_v"].append(v_s)
            st_s["b_conv"].append(buf_s[:, SUBLANES - (B_CONV - 1):])
            st_s["b_h"].append(hl_s[:, 0])
        else:
            w = dict(w_qkv=c_w_in[j][:, :3 * D_MODEL].astype(BF16), w_f=_pad_lanes(c_w_in[j][:, 3 * D_MODEL:]),
                     b_f=_pad_lanes(row(c_b_f[j])), w_out=c_w_out[j].astype(BF16), g1=g1, b1=b1)
            xp, k_p, v_p, lf_p = _fox_layer(xp, None, w)
            past = (cache_c_k[j].reshape(Bs, -1, D_MODEL), cache_c_v[j].reshape(Bs, -1, D_MODEL), cache_c_logf[j])
            xsm, k_s, v_s2, lf_s = _fox_layer(xsm, past, w)
            st_p["c_k"].append(k_p.reshape(Bp, Lp, C_HEADS, C_HEAD_DIM))
            st_p["c_v"].append(v_p.reshape(Bp, Lp, C_HEADS, C_HEAD_DIM))
            st_p["c_logf"].append(lf_p)
            st_s["c_k"].append(k_s.reshape(Bs, Ls, C_HEADS, C_HEAD_DIM))
            st_s["c_v"].append(v_s2.reshape(Bs, Ls, C_HEADS, C_HEAD_DIM))
            st_s["c_logf"].append(lf_s)
        g2, b2 = row(ln2_g[layer]), row(ln2_b[layer])
        wq, wo = w_mq[layer].astype(BF16), w_mo[layer].astype(BF16)
        xp = _mem_layer(xp, p_mem_k[layer], p_mem_v[layer], wq, wo, g2, b2)
        xsm = _mem_layer(xsm, s_mem_k[layer], s_mem_v[layer], wq, wo, g2, b2)
        wm = dict(w_r=_pad_lanes(w_router[layer]), b_r=_pad_lanes(row(b_router[layer])),
                  w_gu=w_gu[layer].astype(BF16), b_gu=b_gu[layer][:, None, :],
                  w_dn=w_dn[layer].astype(BF16), b_dn=b_dn[layer][:, None, :],
                  g3=row(ln3_g[layer]), b3=row(ln3_b[layer]))
        xall = jnp.concatenate([xp.reshape(Tp, D_MODEL), xsm.reshape(Ts, D_MODEL)], axis=0)
        xall, xs_buf = _moe_layer(xall, xs_buf, wm)
        xp = xall[:Tp].reshape(Bp, Lp, D_MODEL)
        xsm = xall[Tp:].reshape(Bs, Ls, D_MODEL)

    return (xp, xsm,
            jnp.stack(st_p["b_conv"]), jnp.stack(st_p["b_h"]),
            jnp.stack(st_p["c_k"]), jnp.stack(st_p["c_v"]), jnp.stack(st_p["c_logf"]),
            p_mem_k.reshape(DEPTH, Bp, N_MEM, M_HEADS, M_HEAD_DIM),
            p_mem_v.reshape(DEPTH, Bp, N_MEM, M_HEADS, M_HEAD_DIM),
            jnp.stack(st_s["a_v"]), jnp.stack(st_s["b_conv"]), jnp.stack(st_s["b_h"]),
            jnp.stack(st_s["c_k"]), jnp.stack(st_s["c_v"]), jnp.stack(st_s["c_logf"]))
```

```python
import functools

import numpy as np
import jax
import jax.numpy as jnp
from jax import lax
from jax.experimental import pallas as pl
from jax.experimental.pallas import tpu as pltpu

F32 = jnp.float32
BF16 = jnp.bfloat16
HIGHEST = lax.Precision.HIGHEST

D_MODEL = 1024
DEPTH = 4
N_ODD = DEPTH // 2
A_WIDTH = 512
A_GROUPS = 4
A_GROUP_DIM = 128
A_CHUNK = 128
B_WIDTH = 512
B_BLOCKS = 8
B_BLOCK_DIM = 64
B_CONV = 4
B_C = 8.0
C_HEADS = 16
C_HEAD_DIM = 64
C_SCALE = C_HEAD_DIM ** -0.5
N_MEM = 256
M_HEADS = 4
M_HEAD_DIM = 128
M_WIDTH = 512
M_SCALE = M_HEAD_DIM ** -0.5
N_EXPERTS = 32
TOP_K = 4
D_EXPERT = 1024
SWIGLU_LIMIT = 7.0
SWIGLU_ALPHA = 1.702
DN_ALPHA = (2 * DEPTH) ** 0.25
LN_EPS = 1e-5

LANES = 128
SUBLANES = 8
VMEM_LIMIT = 56 * 1024 * 1024
NEG_INF = float("-inf")

MOE_TM = 256
ROW_TM = 512
FOX_TM = 256
ATT_T = 256
DISP_TM = 256

SLOT = LANES
QA_W = C_HEADS * SLOT
N_PIECE = 3
ONE_LANE = N_PIECE * C_HEADS


def _cparams(sem):
    return pltpu.CompilerParams(dimension_semantics=sem, vmem_limit_bytes=VMEM_LIMIT)


def _dot(a, b):
    return jnp.dot(a, b, preferred_element_type=F32)


def _dot_nt(a, b):
    return lax.dot_general(a, b, (((1,), (1,)), ((), ())), preferred_element_type=F32)


def _gelu(x):
    return 0.5 * x * (1.0 + jnp.tanh(0.7978845608028654 * (x + 0.044715 * (x * x * x))))


def _sigmoid(x):
    return 1.0 / (1.0 + jnp.exp(-x))


def _log_sigmoid(x):
    return jnp.minimum(x, 0.0) - jnp.log1p(jnp.exp(-jnp.abs(x)))


def _layer_norm(y, g, b):
    mu = jnp.mean(y, axis=-1, keepdims=True)
    d = y - mu
    var = jnp.mean(d * d, axis=-1, keepdims=True)
    return d * lax.rsqrt(var + LN_EPS) * g + b


def _full(shape):
    n = len(shape)
    return pl.BlockSpec(shape, lambda *_: (0,) * n)


def _any():
    return pl.BlockSpec(memory_space=pl.ANY)


def _call_inplace(kernel, **kw):
    return pl.pallas_call(kernel, input_output_aliases={0: 0}, **kw)


def _proj_kernel(x_ref, w_ref, *o_refs):
    xb = x_ref[...].astype(BF16)
    n = o_refs[0].shape[-1]
    for j, o_ref in enumerate(o_refs):
        o_ref[...] = _dot(xb, w_ref[:, j * n:(j + 1) * n])


def _proj(x, w_bf16, n_out=1, tm=ROW_TM):
    T, K = x.shape
    N = w_bf16.shape[1]
    n = N // n_out
    tm = min(tm, T)
    return pl.pallas_call(
        _proj_kernel,
        grid=(T // tm,),
        in_specs=[pl.BlockSpec((tm, K), lambda i: (i, 0)), _full((K, N))],
        out_specs=[pl.BlockSpec((tm, n), lambda i: (i, 0)) for _ in range(n_out)],
        out_shape=[jax.ShapeDtypeStruct((T, n), F32) for _ in range(n_out)],
        compiler_params=_cparams(("parallel",)),
        name="proj",
    )(x, w_bf16)


def _ab_kernel(x_ref, conv0_ref, h0_ref, w_in_ref, lng_ref, lnb_ref, ws_ref, bst_ref,
               cw_ref, cb_ref, wa_ref, ba_ref, wx_ref, bx_ref, lam_ref, w_out_ref,
               g1_ref, b1_ref, x1_ref, *rest, tl, emit_v):
    if emit_v:
        v_ref, buf_ref, hl_ref, xp_s, h_s = rest
    else:
        buf_ref, hl_ref, xp_s, h_s = rest
        v_ref = None
    l = pl.program_id(1)

    @pl.when(l == 0)
    def _():
        xp_s[0:SUBLANES, :] = conv0_ref[...]
        h_s[...] = h0_ref[...]

    x = x_ref[...]
    proj = _dot(x.astype(BF16), w_in_ref[...])
    u = _gelu(proj[:, 0:A_WIDTH])
    v = _gelu(proj[:, A_WIDTH:2 * A_WIDTH])
    y_br = proj[:, 2 * A_WIDTH:2 * A_WIDTH + B_WIDTH]
    x_br = proj[:, 2 * A_WIDTH + B_WIDTH:]

    row = lax.broadcasted_iota(jnp.int32, (tl, tl), 0)
    col = lax.broadcasted_iota(jnp.int32, (tl, tl), 1)
    causal = col <= row
    out = DN_ALPHA * x
    for g in range(A_GROUPS):
        lo, hi = g * A_GROUP_DIM, (g + 1) * A_GROUP_DIM
        vn = _layer_norm(v[:, lo:hi], lng_ref[:, lo:hi], lnb_ref[:, lo:hi])
        if emit_v:
            v_ref[:, lo:hi] = vn
        wg = jnp.where(causal, ws_ref[g, 0:tl, 0:tl], 0.0).astype(BF16)
        z = _dot(wg, vn.astype(BF16)) + bst_ref[0:tl, g:g + 1]
        a_out = u[:, lo:hi] * z
        out = out + _dot(a_out.astype(BF16), w_out_ref[lo:hi, :])

    xp_s[SUBLANES:SUBLANES + tl, :] = x_br
    xc = (cb_ref[...] + cw_ref[3:4, :] * x_br
          + cw_ref[2:3, :] * xp_s[SUBLANES - 1:SUBLANES - 1 + tl, :]
          + cw_ref[1:2, :] * xp_s[SUBLANES - 2:SUBLANES - 2 + tl, :]
          + cw_ref[0:1, :] * xp_s[SUBLANES - 3:SUBLANES - 3 + tl, :])
    tail = xp_s[tl:tl + SUBLANES, :]
    buf_ref[...] = tail
    xp_s[0:SUBLANES, :] = tail

    xcb = xc.astype(BF16)
    r = _sigmoid(_dot(xcb, wa_ref[...]) + ba_ref[...])
    i = _sigmoid(_dot(xcb, wx_ref[...]) + bx_ref[...])
    lam = lam_ref[...]
    softplus_neg_lam = jnp.maximum(-lam, 0.0) + jnp.log1p(jnp.exp(-jnp.abs(lam)))
    log_a = -B_C * r * softplus_neg_lam
    a = jnp.exp(log_a)
    one_minus_a2 = -jnp.tanh(log_a) * (a * a + 1.0)
    bb = (xc * i) * jnp.sqrt(one_minus_a2)

    rows = lax.broadcasted_iota(jnp.int32, (tl, B_WIDTH), 0)
    k = 1
    while k < tl:
        a_sh = pltpu.roll(a, k, 0)
        b_sh = pltpu.roll(bb, k, 0)
        m = rows >= k
        bb = jnp.where(m, a * b_sh + bb, bb)
        a = jnp.where(m, a * a_sh, a)
        k *= 2
    h = a * h_s[...] + bb
    h_last = h[tl - 1:tl, :]
    h_s[...] = h_last
    hl_ref[...] = h_last
    b_out = _gelu(y_br) * h
    out = out + _dot(b_out.astype(BF16), w_out_ref[A_WIDTH:, :])
    x1_ref[...] = _layer_norm(out, g1_ref[...], b1_ref[...])


def _ab_layer(x_all, row_off, B, L, conv0, h0, w, emit_v):
    T = x_all.shape[0]
    tl = min(L, A_CHUNK)
    nl = L // tl
    off = row_off // tl
    rows = lambda n: pl.BlockSpec((tl, n), lambda b, l: (off + b * nl + l, 0))
    local = lambda n: pl.BlockSpec((tl, n), lambda b, l: (b * nl + l, 0))
    st_spec = lambda r: pl.BlockSpec((None, r, B_WIDTH), lambda b, l: (b, 0, 0))
    weights = [w["w_in"], w["ln_g"], w["ln_b"], w["w_s"], w["b_st"], w["conv_w"], w["conv_b"],
               w["wa"], w["ba"], w["wx"], w["bx"], w["lam"], w["w_out"], w["g1"], w["b1"]]
    in_specs = [rows(D_MODEL), st_spec(SUBLANES), st_spec(1)] + [_full(a.shape) for a in weights]
    out_specs = [rows(D_MODEL)] + ([local(A_WIDTH)] if emit_v else []) + [st_spec(SUBLANES), st_spec(1)]
    out_shape = ([jax.ShapeDtypeStruct((T, D_MODEL), F32)]
                 + ([jax.ShapeDtypeStruct((B * L, A_WIDTH), F32)] if emit_v else [])
                 + [jax.ShapeDtypeStruct((B, SUBLANES, B_WIDTH), F32),
                    jax.ShapeDtypeStruct((B, 1, B_WIDTH), F32)])
    call = _call_inplace(
        functools.partial(_ab_kernel, tl=tl, emit_v=emit_v),
        grid=(B, nl), in_specs=in_specs, out_specs=out_specs, out_shape=out_shape,
        scratch_shapes=[pltpu.VMEM((tl + SUBLANES, B_WIDTH), F32), pltpu.VMEM((1, B_WIDTH), F32)],
        compiler_params=_cparams(("parallel", "arbitrary")), name="ab_mixer")
    return call(x_all, conv0, h0, *weights)


def _out_ln_kernel(x_ref, o_ref, w_ref, g_ref, b_ref, y_ref):
    mix = _dot(o_ref[...].astype(BF16), w_ref[...])
    y_ref[...] = _layer_norm(DN_ALPHA * x_ref[...] + mix, g_ref[...], b_ref[...])


def _out_ln(x_all, row_off, o, w_bf16, g, b, tm=ROW_TM):
    T = x_all.shape[0]
    n, K = o.shape
    tm = min(tm, n)
    off = row_off // tm
    rows = pl.BlockSpec((tm, D_MODEL), lambda i: (off + i, 0))
    in_specs = [rows, pl.BlockSpec((tm, K), lambda i: (i, 0)), _full(w_bf16.shape), _full(g.shape), _full(b.shape)]
    call = _call_inplace(
        _out_ln_kernel,
        grid=(n // tm,), in_specs=in_specs, out_specs=rows,
        out_shape=jax.ShapeDtypeStruct((T, D_MODEL), F32),
        compiler_params=_cparams(("parallel",)), name="out_ln")
    return call(x_all, o, w_bf16, g, b)


def _fox_pre_p_kernel(x_ref, wkt_ref, wvt_ref, wqa_ref, wka_ref, wf_ref, bf_ref, pq_ref, pk_ref,
                      kt_ref, vt_ref, lft_ref, qa_ref, ka_ref, carry_s, *, tm):
    @pl.when(pl.program_id(1) == 0)
    def _():
        carry_s[...] = jnp.zeros(carry_s.shape, F32)

    x = x_ref[...]
    xb = x.astype(BF16)
    kt_ref[...] = _dot_nt(wkt_ref[...], xb)
    vt_ref[...] = _dot_nt(wvt_ref[...], xb)

    x_lo = (x - xb.astype(F32)).astype(BF16)
    wf = wf_ref[...]
    wf_hi = wf.astype(BF16)
    wf_lo = (wf - wf_hi.astype(F32)).astype(BF16)
    f = _dot(xb, wf_hi) + _dot(x_lo, wf_hi) + _dot(xb, wf_lo) + bf_ref[...]
    lane = lax.broadcasted_iota(jnp.int32, (tm, LANES), 1)
    logf = jnp.where(lane < C_HEADS, _log_sigmoid(f), 0.0)
    lft_ref[...] = logf.T[0:C_HEADS, :]

    c = logf
    rows = lax.broadcasted_iota(jnp.int32, (tm, LANES), 0)
    k = 1
    while k < tm:
        c = c + jnp.where(rows >= k, pltpu.roll(c, k, 0), 0.0)
        k *= 2
    c = c + carry_s[...]
    carry_s[...] = c[tm - 1:tm, :]

    hi = c.astype(BF16).astype(F32)
    r1 = c - hi
    mid = r1.astype(BF16).astype(F32)
    lo = (r1 - mid).astype(BF16).astype(F32)
    pieces = (hi + pltpu.roll(mid, C_HEADS, 1) + pltpu.roll(lo, 2 * C_HEADS, 1)
              + jnp.where(lane == ONE_LANE, 1.0, 0.0)).astype(BF16)
    qa_ref[...] = (_dot(xb, wqa_ref[...]) + _dot(pieces, pq_ref[...])).astype(BF16)
    ka_ref[...] = (_dot(xb, wka_ref[...]) + _dot(pieces, pk_ref[...])).astype(BF16)


def _piece_placement():
    pq = np.zeros((LANES, QA_W), np.float32)
    pk = np.zeros((LANES, QA_W), np.float32)
    for h in range(C_HEADS):
        base = h * SLOT + C_HEAD_DIM
        for p in range(N_PIECE):
            pq[p * C_HEADS + h, base + p] = 1.0
            pq[ONE_LANE, base + N_PIECE + p] = 1.0
            pk[ONE_LANE, base + p] = 1.0
            pk[p * C_HEADS + h, base + N_PIECE + p] = -1.0
    return jnp.asarray(pq, BF16), jnp.asarray(pk, BF16)


def _slot_weights(w, scale):
    w = (w * scale).reshape(D_MODEL, C_HEADS, C_HEAD_DIM)
    w = jnp.pad(w, ((0, 0), (0, 0), (0, SLOT - C_HEAD_DIM)))
    return w.reshape(D_MODEL, QA_W).astype(BF16)


def _fox_pre_p(x_all, B, L, w, j, kt_prev, vt_prev, lft_prev, tm=FOX_TM):
    nl = L // tm
    weights = [w["wkt"], w["wvt"], w["wqa"], w["wka"], w["w_f"], w["b_f"], w["pq"], w["pk"]]
    in_specs = [pl.BlockSpec((tm, D_MODEL), lambda b, l: (b * nl + l, 0))] + [_full(a.shape) for a in weights]
    t_spec = lambda n: pl.BlockSpec((None, None, n, tm), lambda b, l: (j, b, 0, l))
    a_spec = pl.BlockSpec((tm, QA_W), lambda b, l: (b * nl + l, 0))
    out_specs = [t_spec(D_MODEL), t_spec(D_MODEL), t_spec(C_HEADS), a_spec, a_spec]
    out_shape = [jax.ShapeDtypeStruct((N_ODD, B, D_MODEL, L), F32), jax.ShapeDtypeStruct((N_ODD, B, D_MODEL, L), F32),
                 jax.ShapeDtypeStruct((N_ODD, B, C_HEADS, L), F32),
                 jax.ShapeDtypeStruct((B * L, QA_W), BF16), jax.ShapeDtypeStruct((B * L, QA_W), BF16)]
    kern = functools.partial(_fox_pre_p_kernel, tm=tm)
    kw = dict(grid=(B, nl), out_specs=out_specs, out_shape=out_shape,
              scratch_shapes=[pltpu.VMEM((1, LANES), F32)],
              compiler_params=_cparams(("parallel", "arbitrary")), name="fox_pre_p")
    n_in = len(in_specs)

    def with_prev(*refs):
        return kern(*refs[:n_in], *refs[n_in + 3:])
    return pl.pallas_call(with_prev, in_specs=in_specs + [_any()] * 3,
                          input_output_aliases={n_in: 0, n_in + 1: 1, n_in + 2: 2}, **kw)(
        x_all, *weights, kt_prev, vt_prev, lft_prev)


def _fox_attn_p_kernel(qa_ref, ka_ref, vt_ref, o_ref, acc_s, *, t):
    qi = pl.program_id(2)
    qas = [qa_ref[:, hh * SLOT:(hh + 1) * SLOT] for hh in range(2)]
    acc_s[...] = jnp.zeros(acc_s.shape, F32)
    krow = lax.broadcasted_iota(jnp.int32, (t, t), 0)
    qcol = lax.broadcasted_iota(jnp.int32, (t, t), 1)

    def scores(off):
        return tuple(_dot_nt(ka_ref[pl.ds(off, t), hh * SLOT:(hh + 1) * SLOT], qas[hh]) for hh in range(2))

    def consume(off, ss, carry, masked):
        new = []
        for hh in range(2):
            m_old, l_old = carry[2 * hh], carry[2 * hh + 1]
            s = ss[hh]
            if masked:
                s = jnp.where(krow <= qcol, s, NEG_INF)
            m_new = jnp.maximum(m_old, jnp.max(s, axis=0, keepdims=True))
            p = jnp.exp(s - m_new)
            alpha = jnp.exp(m_old - m_new)
            l_new = alpha * l_old + jnp.sum(p, axis=0, keepdims=True)
            vc = vt_ref[hh * C_HEAD_DIM:(hh + 1) * C_HEAD_DIM, pl.ds(off, t)].astype(BF16)
            acc_s[hh] = alpha * acc_s[hh] + _dot(vc, p.astype(BF16))
            new += [m_new, l_new]
        return tuple(new)

    def body(jj, c):
        nxt = scores(pl.multiple_of((jj + 1) * t, t))
        return consume(pl.multiple_of(jj * t, t), c[4:], c[:4], False) + nxt

    init = (jnp.full((1, t), NEG_INF, F32), jnp.zeros((1, t), F32)) * 2
    c = lax.fori_loop(0, qi, body, init + scores(0))
    carry = consume(pl.multiple_of(qi * t, t), c[4:], c[:4], True)
    o_t = jnp.concatenate([acc_s[0] / carry[1], acc_s[1] / carry[3]], axis=0)
    o_ref[...] = o_t.T


def _fox_attn_p(qa, ka, vt, j, B, L, t=ATT_T):
    nq = L // t
    n_hp = C_HEADS // 2
    return pl.pallas_call(
        functools.partial(_fox_attn_p_kernel, t=t),
        grid=(B, n_hp, nq),
        in_specs=[pl.BlockSpec((t, 2 * SLOT), lambda b, h, i: (b * nq + i, h)),
                  pl.BlockSpec((L, 2 * SLOT), lambda b, h, i: (b, h)),
                  pl.BlockSpec((None, None, 2 * C_HEAD_DIM, L), lambda b, h, i: (j, b, h, 0))],
        out_specs=pl.BlockSpec((t, LANES), lambda b, h, i: (b * nq + i, h)),
        out_shape=jax.ShapeDtypeStruct((B * L, D_MODEL), F32),
        scratch_shapes=[pltpu.VMEM((2, C_HEAD_DIM, t), F32)],
        compiler_params=_cparams(("parallel", "parallel", "arbitrary")),
        name="fox_attn_p",
    )(qa, ka, vt)


def _fox_pre_s_kernel(x_ref, w_ref, wf_ref, bf_ref, q_ref, k_ref, v_ref, lf_ref):
    x = x_ref[...]
    xb = x.astype(BF16)
    q_ref[...] = _dot(xb, w_ref[:, 0:D_MODEL])
    k_ref[...] = _dot(xb, w_ref[:, D_MODEL:2 * D_MODEL])
    v_ref[...] = _dot(xb, w_ref[:, 2 * D_MODEL:3 * D_MODEL])
    f = jnp.dot(x, wf_ref[...], preferred_element_type=F32, precision=HIGHEST) + bf_ref[...]
    lf_ref[...] = _log_sigmoid(f)[:, 0:C_HEADS]


def _fox_pre_s(x_all, row_off, n, w_qkv, w_f, b_f, tm=ROW_TM):
    tm = min(tm, n)
    off = row_off // tm
    row = lambda c: pl.BlockSpec((tm, c), lambda i: (i, 0))
    return pl.pallas_call(
        _fox_pre_s_kernel,
        grid=(n // tm,),
        in_specs=[pl.BlockSpec((tm, D_MODEL), lambda i: (off + i, 0)),
                  _full(w_qkv.shape), _full(w_f.shape), _full(b_f.shape)],
        out_specs=[row(D_MODEL), row(D_MODEL), row(D_MODEL), row(C_HEADS)],
        out_shape=[jax.ShapeDtypeStruct((n, D_MODEL), F32)] * 3 + [jax.ShapeDtypeStruct((n, C_HEADS), F32)],
        compiler_params=_cparams(("parallel",)),
        name="fox_pre_s",
    )(x_all, w_qkv, w_f, b_f)


def _cumsum_kernel(x_ref, o_ref):
    x = x_ref[...]
    n = x.shape[-1]
    lane = lax.broadcasted_iota(jnp.int32, x.shape, 1)
    k = 1
    while k < n:
        x = x + jnp.where(lane >= k, pltpu.roll(x, k, 1), 0.0)
        k *= 2
    o_ref[...] = x


def _cumsum_lanes(x):
    B, H, Lp = x.shape
    spec = pl.BlockSpec((None, H, Lp), lambda b: (b, 0, 0))
    return pl.pallas_call(
        _cumsum_kernel, grid=(B,), in_specs=[spec], out_specs=spec,
        out_shape=jax.ShapeDtypeStruct(x.shape, F32),
        compiler_params=_cparams(("parallel",)), name="logf_cumsum",
    )(x)


def _fox_attn_s_kernel(q_ref, kn_ref, vn_ref, kt_ref, vt_ref, cq_ref, ck_ref, o_ref, *, L, P):
    row = lax.broadcasted_iota(jnp.int32, (2 * L, LANES), 0)
    lane = lax.broadcasted_iota(jnp.int32, (2 * L, LANES), 1)
    q = q_ref[...] * C_SCALE
    q2 = jnp.concatenate([q, q], axis=0)
    qb = jnp.where((row < L) == (lane < C_HEAD_DIM), q2, 0.0).astype(BF16)
    s_p = _dot(qb, kt_ref[...].astype(BF16))
    s_n = _dot_nt(qb, kn_ref[...].astype(BF16))
    cq2 = jnp.concatenate([cq_ref[0], cq_ref[1]], axis=0)
    first_p = lax.broadcasted_iota(jnp.int32, (2 * L, P), 0) < L
    s_p = s_p + (cq2 - jnp.where(first_p, ck_ref[0:1, 0:P], ck_ref[1:2, 0:P]))
    rn = lax.broadcasted_iota(jnp.int32, (2 * L, L), 0)
    cn = lax.broadcasted_iota(jnp.int32, (2 * L, L), 1)
    s_n = s_n + (cq2 - jnp.where(rn < L, ck_ref[0:1, P:P + L], ck_ref[1:2, P:P + L]))
    s_n = jnp.where(cn <= jnp.where(rn < L, rn, rn - L), s_n, NEG_INF)
    m = jnp.maximum(jnp.max(s_p, axis=-1, keepdims=True), jnp.max(s_n, axis=-1, keepdims=True))
    p_p = jnp.exp(s_p - m)
    p_n = jnp.exp(s_n - m)
    den = jnp.sum(p_p, axis=-1, keepdims=True) + jnp.sum(p_n, axis=-1, keepdims=True)
    o2 = (_dot_nt(p_p.astype(BF16), vt_ref[...].astype(BF16))
          + _dot(p_n.astype(BF16), vn_ref[...].astype(BF16))) / den
    first_lanes = lax.broadcasted_iota(jnp.int32, (L, LANES), 1) < C_HEAD_DIM
    o_ref[...] = jnp.where(first_lanes, o2[0:L], o2[L:2 * L])


def _fox_attn_s(q, k, v, kt_cache, vt_cache, j, cq, ck, B, L):
    P = kt_cache.shape[-1]
    n_hp = C_HEADS // 2
    row = pl.BlockSpec((L, LANES), lambda b, h: (b, h))
    cache = pl.BlockSpec((None, None, 2 * C_HEAD_DIM, P), lambda b, h: (j, b, h, 0))
    return pl.pallas_call(
        functools.partial(_fox_attn_s_kernel, L=L, P=P),
        grid=(B, n_hp),
        in_specs=[row, row, row, cache, cache,
                  pl.BlockSpec((None, None, 2, L, 1), lambda b, h: (b, h, 0, 0, 0)),
                  pl.BlockSpec((None, None, 2, ck.shape[-1]), lambda b, h: (b, h, 0, 0))],
        out_specs=row,
        out_shape=jax.ShapeDtypeStruct((B * L, D_MODEL), F32),
        compiler_params=_cparams(("parallel", "parallel")),
        name="fox_attn_s",
    )(q, k, v, kt_cache, vt_cache, cq, ck)


def _mem_kernel(x_ref, mk_ref, mv_ref, wq_ref, wo_ref, g_ref, b_ref, y_ref):
    x = x_ref[...]
    q = _dot(x.astype(BF16), wq_ref[...])
    out = DN_ALPHA * x
    for h in range(M_HEADS):
        lo, hi = h * M_HEAD_DIM, (h + 1) * M_HEAD_DIM
        s = _dot_nt(q[:, lo:hi].astype(BF16), mk_ref[:, lo:hi].astype(BF16)) * M_SCALE
        s = s - jnp.max(s, axis=-1, keepdims=True)
        p = jnp.exp(s)
        p = p / jnp.sum(p, axis=-1, keepdims=True)
        o = _dot(p.astype(BF16), mv_ref[:, lo:hi].astype(BF16))
        out = out + _dot(o.astype(BF16), wo_ref[lo:hi, :])
    y_ref[...] = _layer_norm(out, g_ref[...], b_ref[...])


def _mem_layer(x_all, row_off, B, L, mk, mv, layer, w_q, w_o, g, b, tl=256):
    T = x_all.shape[0]
    tl = min(tl, L)
    nl = L // tl
    off = row_off // tl
    rows = pl.BlockSpec((tl, D_MODEL), lambda bb, l: (off + bb * nl + l, 0))
    mem = pl.BlockSpec((None, None, N_MEM, M_WIDTH), lambda bb, l: (layer, bb, 0, 0))
    in_specs = [rows, mem, mem, _full(w_q.shape), _full(w_o.shape), _full(g.shape), _full(b.shape)]
    call = _call_inplace(
        _mem_kernel,
        grid=(B, nl), in_specs=in_specs, out_specs=rows,
        out_shape=jax.ShapeDtypeStruct((T, D_MODEL), F32),
        compiler_params=_cparams(("parallel", "parallel")), name="mem_attend")
    return call(x_all, mk, mv, w_q, w_o, g, b)


def _router_kernel(x_ref, wr_ref, br_ref, mi_ref, mw_ref, cnt_ref, carry_s, *, tm):
    @pl.when(pl.program_id(0) == 0)
    def _():
        carry_s[...] = jnp.zeros(carry_s.shape, F32)

    x = x_ref[...]
    logits = jnp.dot(x, wr_ref[...], preferred_element_type=F32, precision=HIGHEST) + br_ref[...]
    lane = lax.broadcasted_iota(jnp.int32, (tm, LANES), 1)
    work = jnp.where(lane < N_EXPERTS, logits, NEG_INF)
    vals, idxs = [], []
    for _ in range(TOP_K):
        mx = jnp.max(work, axis=-1, keepdims=True)
        idx = jnp.min(jnp.where(work == mx, lane, LANES), axis=-1, keepdims=True)
        vals.append(mx)
        idxs.append(idx)
        work = jnp.where(lane == idx, NEG_INF, work)
    es = [jnp.exp(vv - vals[0]) for vv in vals]
    den = es[0] + es[1] + es[2] + es[3]
    sel = jnp.zeros((tm, LANES), F32)
    for idx in idxs:
        sel = jnp.where(lane == idx, 1.0, sel)
    rr = lax.broadcasted_iota(jnp.int32, (tm, tm), 0)
    cc = lax.broadcasted_iota(jnp.int32, (tm, tm), 1)
    strict = jnp.where(cc < rr, 1.0, 0.0).astype(BF16)
    rank = carry_s[...] + _dot(strict, sel.astype(BF16))
    mi = jnp.zeros((tm, LANES), jnp.int32)
    mw = jnp.zeros((tm, LANES), F32)
    for kk in range(TOP_K):
        rk = jnp.sum(jnp.where(lane == idxs[kk], rank, 0.0), axis=-1, keepdims=True)
        mi = jnp.where(lane == kk, idxs[kk], mi)
        mi = jnp.where(lane == TOP_K + kk, rk.astype(jnp.int32), mi)
        mw = jnp.where(lane == kk, es[kk] / den, mw)
    mi_ref[...] = mi
    mw_ref[...] = mw
    total = carry_s[...] + jnp.sum(sel, axis=0, keepdims=True)
    carry_s[...] = total
    cnt_ref[...] = total


def _router(x, w_r, b_r, tm=ROW_TM):
    T = x.shape[0]
    tm = min(tm, T)
    row = lambda: pl.BlockSpec((tm, LANES), lambda i: (i, 0))
    return pl.pallas_call(
        functools.partial(_router_kernel, tm=tm),
        grid=(T // tm,),
        in_specs=[pl.BlockSpec((tm, D_MODEL), lambda i: (i, 0)), _full(w_r.shape), _full(b_r.shape)],
        out_specs=[row(), row(), _full((1, LANES))],
        out_shape=[jax.ShapeDtypeStruct((T, LANES), jnp.int32), jax.ShapeDtypeStruct((T, LANES), F32),
                   jax.ShapeDtypeStruct((1, LANES), F32)],
        scratch_shapes=[pltpu.VMEM((1, LANES), F32)],
        compiler_params=_cparams(("arbitrary",)),
        name="router",
    )(x, w_r, b_r)


def _row_copy(src, src_row, dst, dst_row, sem):
    return pltpu.make_async_copy(src.at[pl.ds(src_row, 1)], dst.at[pl.ds(dst_row, 1)], sem)


def _dispatch_kernel(pos_ref, x_ref, xs_in_ref, xs_ref, sem, *, tm):
    del xs_in_ref
    base = pl.program_id(0) * tm * TOP_K

    def issue(t, c):
        for kk in range(TOP_K):
            _row_copy(x_ref, t, xs_ref, pos_ref[base + t * TOP_K + kk], sem).start()
        return c
    lax.fori_loop(0, tm, issue, 0)

    def drain(t, c):
        for kk in range(TOP_K):
            _row_copy(x_ref, 0, xs_ref, 0, sem).wait()
        return c
    lax.fori_loop(0, tm, drain, 0)


def _dispatch(pos_flat, x, xs_buf, tm=DISP_TM):
    T = x.shape[0]
    tm = min(tm, T)
    return pl.pallas_call(
        functools.partial(_dispatch_kernel, tm=tm),
        grid_spec=pltpu.PrefetchScalarGridSpec(
            num_scalar_prefetch=1,
            grid=(T // tm,),
            in_specs=[pl.BlockSpec((tm, D_MODEL), lambda i, pos: (i, 0)), _any()],
            out_specs=_any(),
            scratch_shapes=[pltpu.SemaphoreType.DMA(())],
        ),
        out_shape=jax.ShapeDtypeStruct(xs_buf.shape, F32),
        input_output_aliases={2: 0},
        compiler_params=_cparams(("arbitrary",)),
        name="moe_dispatch",
    )(pos_flat, x, xs_buf)


def _gmm_kernel(te_ref, nu_ref, xs_ref, wgu_ref, bgu_ref, wdn_ref, bdn_ref, ys_ref, wgu_s, wdn_s):
    i = pl.program_id(0)

    @pl.when((i == 0) | (te_ref[i] != te_ref[jnp.maximum(i - 1, 0)]))
    def _():
        def cast_rows(c, carry):
            r = pl.multiple_of(c * LANES, LANES)
            wgu_s[pl.ds(r, LANES), :] = wgu_ref[pl.ds(r, LANES), :].astype(BF16)
            wdn_s[pl.ds(r, LANES), :] = wdn_ref[pl.ds(r, LANES), :].astype(BF16)
            return carry
        lax.fori_loop(0, D_MODEL // LANES, cast_rows, 0)

    @pl.when(i < nu_ref[0])
    def _():
        xb = xs_ref[...].astype(BF16)
        gu = _dot(xb, wgu_s[...]) + bgu_ref[...]
        g = jnp.minimum(gu[:, 0:D_EXPERT], SWIGLU_LIMIT)
        u = jnp.clip(gu[:, D_EXPERT:], -SWIGLU_LIMIT, SWIGLU_LIMIT)
        h = (u + 1.0) * g * _sigmoid(SWIGLU_ALPHA * g)
        ys_ref[...] = _dot(h.astype(BF16), wdn_s[...]) + bdn_ref[...]

    @pl.when(i >= nu_ref[0])
    def _():
        ys_ref[...] = jnp.zeros(ys_ref.shape, F32)


def _gmm(tile_expert, n_used, xs, w_gu, b_gu, w_dn, b_dn, layer, tm=MOE_TM):
    R = xs.shape[0]
    wmap = lambda i, te, nu: (layer, te[i], 0, 0)
    return pl.pallas_call(
        _gmm_kernel,
        grid_spec=pltpu.PrefetchScalarGridSpec(
            num_scalar_prefetch=2,
            grid=(R // tm,),
            in_specs=[pl.BlockSpec((tm, D_MODEL), lambda i, te, nu: (i, 0)),
                      pl.BlockSpec((None, None, D_MODEL, 2 * D_EXPERT), wmap),
                      pl.BlockSpec((None, None, 1, 2 * D_EXPERT), wmap),
                      pl.BlockSpec((None, None, D_EXPERT, D_MODEL), wmap),
                      pl.BlockSpec((None, None, 1, D_MODEL), wmap)],
            out_specs=pl.BlockSpec((tm, D_MODEL), lambda i, te, nu: (i, 0)),
            scratch_shapes=[pltpu.VMEM((D_MODEL, 2 * D_EXPERT), BF16), pltpu.VMEM((D_EXPERT, D_MODEL), BF16)],
        ),
        out_shape=jax.ShapeDtypeStruct((R, D_MODEL), F32),
        compiler_params=_cparams(("arbitrary",)),
        name="moe_gmm",
    )(tile_expert, n_used, xs, w_gu, b_gu, w_dn, b_dn)


def _combine_kernel(pos_ref, x_ref, mw_ref, ys_ref, g_ref, b_ref, y_ref, buf, sem, *, tm):
    base = pl.program_id(0) * tm * TOP_K

    def issue(t, c):
        for kk in range(TOP_K):
            _row_copy(ys_ref, pos_ref[base + t * TOP_K + kk], buf.at[kk], t, sem).start()
        return c
    lax.fori_loop(0, tm, issue, 0)

    def drain(t, c):
        for kk in range(TOP_K):
            _row_copy(ys_ref, 0, buf.at[kk], 0, sem).wait()
        return c
    lax.fori_loop(0, tm, drain, 0)

    out = DN_ALPHA * x_ref[...]
    for kk in range(TOP_K):
        out = out + mw_ref[:, kk:kk + 1] * buf[kk]
    y_ref[...] = _layer_norm(out, g_ref[...], b_ref[...])


def _combine(pos_flat, x, meta_w, ys, g, b, tm=DISP_TM):
    T = x.shape[0]
    tm = min(tm, T)
    return pl.pallas_call(
        functools.partial(_combine_kernel, tm=tm),
        grid_spec=pltpu.PrefetchScalarGridSpec(
            num_scalar_prefetch=1,
            grid=(T // tm,),
            in_specs=[pl.BlockSpec((tm, D_MODEL), lambda i, pos: (i, 0)),
                      pl.BlockSpec((tm, LANES), lambda i, pos: (i, 0)),
                      _any(),
                      pl.BlockSpec((1, D_MODEL), lambda i, pos: (0, 0)),
                      pl.BlockSpec((1, D_MODEL), lambda i, pos: (0, 0))],
            out_specs=pl.BlockSpec((tm, D_MODEL), lambda i, pos: (i, 0)),
            scratch_shapes=[pltpu.VMEM((TOP_K, tm, D_MODEL), F32), pltpu.SemaphoreType.DMA(())],
        ),
        out_shape=jax.ShapeDtypeStruct((T, D_MODEL), F32),
        compiler_params=_cparams(("arbitrary",)),
        name="moe_combine",
    )(pos_flat, x, meta_w, ys, g, b)


def _moe_layer(x, xs_buf, w, layer, tm=MOE_TM):
    T = x.shape[0]
    R = xs_buf.shape[0]
    n_tiles = R // tm
    meta_i, meta_w, counts = _router(x, w["w_r"], w["b_r"])
    cnt = counts[0, :N_EXPERTS].astype(jnp.int32)
    tiles_per = (cnt + tm - 1) // tm
    tile_end = jnp.cumsum(tiles_per)
    group_start = (tile_end - tiles_per) * tm
    n_used = tile_end[-1:]
    tile_ids = jnp.arange(n_tiles, dtype=jnp.int32)
    tile_expert = jnp.sum(tile_ids[:, None] >= tile_end[None, :], axis=1).astype(jnp.int32)
    last_expert = jnp.sum(jnp.maximum(n_used - 1, 0) >= tile_end).astype(jnp.int32)
    tile_expert = jnp.where(tile_ids < n_used, tile_expert, last_expert)
    pos = group_start[meta_i[:, :TOP_K]] + meta_i[:, TOP_K:2 * TOP_K]
    pos_flat = pos.reshape(T * TOP_K).astype(jnp.int32)
    xs = _dispatch(pos_flat, x, xs_buf)
    ys = _gmm(tile_expert, n_used.astype(jnp.int32), xs, w["w_gu"], w["b_gu"], w["w_dn"], w["b_dn"], layer, tm)
    return _combine(pos_flat, x, meta_w, ys, w["g3"], w["b3"]), xs


def _block_diag(w):
    eye = jnp.eye(B_BLOCKS, dtype=w.dtype)
    return jnp.einsum("hij,hg->higj", w, eye).reshape(B_WIDTH, B_WIDTH)


def _pad_lanes(a, value=0.0):
    return jnp.pad(a, ((0, 0),) * (a.ndim - 1) + ((0, LANES - a.shape[-1]),), constant_values=value)


def kernel(x_prompt, x_sample, state_b_conv, state_b_h, cache_c_k, cache_c_v, cache_c_logf, cache_mem_k, cache_mem_v, mem_prompt, w_in_ab, a_ln_g, a_ln_b, a_w_s, a_b_s, b_conv_w, b_conv_b, b_w_a, b_b_a, b_w_x, b_b_x, b_lambda, w_out_ab, c_w_in, c_b_f, c_w_out, w_mq, w_mk, w_mv, w_mo, w_router, b_router, w_gu, b_gu, w_dn, b_dn, ln1_g, ln1_b, ln2_g, ln2_b, ln3_g, ln3_b):
    Bp, Lp, _ = x_prompt.shape
    Bs, Ls, _ = x_sample.shape
    P = cache_c_k.shape[2]
    Tp, Ts = Bp * Lp, Bs * Ls
    T = Tp + Ts
    row = lambda a: a.reshape(1, -1)

    mem_flat = mem_prompt.reshape(Bp * N_MEM, D_MODEL)
    w_mkv = jnp.concatenate([w_mk, w_mv], axis=-1).astype(BF16)
    p_mem = [_proj(mem_flat, w_mkv[l], n_out=2) for l in range(DEPTH)]
    p_mem_k = jnp.stack([p[0] for p in p_mem]).reshape(DEPTH, Bp, N_MEM, M_WIDTH)
    p_mem_v = jnp.stack([p[1] for p in p_mem]).reshape(DEPTH, Bp, N_MEM, M_WIDTH)
    s_mem_k = cache_mem_k.reshape(DEPTH, Bs, N_MEM, M_WIDTH)
    s_mem_v = cache_mem_v.reshape(DEPTH, Bs, N_MEM, M_WIDTH)

    kt_cache = jnp.transpose(cache_c_k, (0, 1, 3, 4, 2)).reshape(N_ODD, Bs, D_MODEL, P)
    vt_cache = jnp.transpose(cache_c_v, (0, 1, 3, 4, 2)).reshape(N_ODD, Bs, D_MODEL, P)
    lft_cache = jnp.transpose(cache_c_logf, (0, 1, 3, 2))
    pq, pk = _piece_placement()

    b_gu4 = b_gu[:, :, None, :]
    b_dn4 = b_dn[:, :, None, :]
    R = T * TOP_K + N_EXPERTS * MOE_TM
    xs_buf = jnp.zeros((R, D_MODEL), F32)

    x_all = jnp.concatenate([x_prompt.reshape(Tp, D_MODEL), x_sample.reshape(Ts, D_MODEL)], axis=0)
    kt_p = jnp.zeros((N_ODD, Bp, D_MODEL, Lp), F32)
    vt_p = jnp.zeros((N_ODD, Bp, D_MODEL, Lp), F32)
    lft_p = jnp.zeros((N_ODD, Bp, C_HEADS, Lp), F32)
    st_p = {"b_conv": [], "b_h": []}
    st_s = {"a_v": [], "b_conv": [], "b_h": [], "c_k": [], "c_v": [], "c_logf": []}
    for layer in range(DEPTH):
        j = layer // 2
        g1, b1 = row(ln1_g[layer]), row(ln1_b[layer])
        if layer % 2 == 0:
            w = dict(w_in=w_in_ab[j].astype(BF16), ln_g=row(a_ln_g[j]), ln_b=row(a_ln_b[j]), w_s=a_w_s[j],
                     b_st=a_b_s[j].T, conv_w=b_conv_w[j], conv_b=row(b_conv_b[j]),
                     wa=_block_diag(b_w_a[j]).astype(BF16), ba=row(b_b_a[j]),
                     wx=_block_diag(b_w_x[j]).astype(BF16), bx=row(b_b_x[j]), lam=row(b_lambda[j]),
                     w_out=w_out_ab[j].astype(BF16), g1=g1, b1=b1)
            conv0_p = jnp.zeros((Bp, SUBLANES, B_WIDTH), F32)
            h0_p = jnp.zeros((Bp, 1, B_WIDTH), F32)
            x_all, buf_p, hl_p = _ab_layer(x_all, 0, Bp, Lp, conv0_p, h0_p, w, False)
            conv0_s = jnp.pad(state_b_conv[j], ((0, 0), (SUBLANES - (B_CONV - 1), 0), (0, 0)))
            x_all, v_s, buf_s, hl_s = _ab_layer(x_all, Tp, Bs, Ls, conv0_s, state_b_h[j][:, None, :], w, True)
            st_p["b_conv"].append(buf_p[:, SUBLANES - (B_CONV - 1):])
            st_p["b_h"].append(hl_p[:, 0])
            st_s["a_v"].append(v_s.reshape(Bs, Ls, A_WIDTH))
            st_s["b_conv"].append(buf_s[:, SUBLANES - (B_CONV - 1):])
            st_s["b_h"].append(hl_s[:, 0])
        else:
            wq, wk, wv = (c_w_in[j][:, i * D_MODEL:(i + 1) * D_MODEL] for i in range(3))
            w_f = _pad_lanes(c_w_in[j][:, 3 * D_MODEL:])
            b_f = _pad_lanes(row(c_b_f[j]))
            w_out = c_w_out[j].astype(BF16)
            wp = dict(wkt=wk.T.astype(BF16), wvt=wv.T.astype(BF16), wqa=_slot_weights(wq, C_SCALE),
                      wka=_slot_weights(wk, 1.0), w_f=w_f, b_f=b_f, pq=pq, pk=pk)
            kt_p, vt_p, lft_p, qa, ka = _fox_pre_p(x_all, Bp, Lp, wp, j, kt_p, vt_p, lft_p)
            o_p = _fox_attn_p(qa, ka, vt_p, j, Bp, Lp)
            q_s, k_s, v_s2, lf_s = _fox_pre_s(x_all, Tp, Ts, c_w_in[j][:, :3 * D_MODEL].astype(BF16), w_f, b_f)
            tot = P + Ls
            tot_pad = -(-tot // LANES) * LANES
            lf_t = jnp.concatenate([lft_cache[j], lf_s.reshape(Bs, Ls, C_HEADS).transpose(0, 2, 1),
                                    jnp.zeros((Bs, C_HEADS, tot_pad - tot), F32)], axis=2)
            cum = _cumsum_lanes(lf_t)
            ck = cum.reshape(Bs, C_HEADS // 2, 2, tot_pad)
            cq = cum[:, :, P:P + Ls].reshape(Bs, C_HEADS // 2, 2, Ls, 1)
            o_s = _fox_attn_s(q_s, k_s, v_s2, kt_cache, vt_cache, j, cq, ck, Bs, Ls)
            x_all = _out_ln(x_all, 0, o_p, w_out, g1, b1)
            x_all = _out_ln(x_all, Tp, o_s, w_out, g1, b1)
            st_s["c_k"].append(k_s.reshape(Bs, Ls, C_HEADS, C_HEAD_DIM))
            st_s["c_v"].append(v_s2.reshape(Bs, Ls, C_HEADS, C_HEAD_DIM))
            st_s["c_logf"].append(lf_s.reshape(Bs, Ls, C_HEADS))
        g2, b2 = row(ln2_g[layer]), row(ln2_b[layer])
        wq_m, wo_m = w_mq[layer].astype(BF16), w_mo[layer].astype(BF16)
        x_all = _mem_layer(x_all, 0, Bp, Lp, p_mem_k, p_mem_v, layer, wq_m, wo_m, g2, b2)
        x_all = _mem_layer(x_all, Tp, Bs, Ls, s_mem_k, s_mem_v, layer, wq_m, wo_m, g2, b2)
        wm = dict(w_r=_pad_lanes(w_router[layer]), b_r=_pad_lanes(row(b_router[layer])),
                  w_gu=w_gu, b_gu=b_gu4, w_dn=w_dn, b_dn=b_dn4,
                  g3=row(ln3_g[layer]), b3=row(ln3_b[layer]))
        x_all, xs_buf = _moe_layer(x_all, xs_buf, wm, layer)

    p_c_k = jnp.transpose(kt_p.reshape(N_ODD, Bp, C_HEADS, C_HEAD_DIM, Lp), (0, 1, 4, 2, 3))
    p_c_v = jnp.transpose(vt_p.reshape(N_ODD, Bp, C_HEADS, C_HEAD_DIM, Lp), (0, 1, 4, 2, 3))
    p_c_logf = jnp.transpose(lft_p, (0, 1, 3, 2))
    return (x_all[:Tp].reshape(Bp, Lp, D_MODEL), x_all[Tp:].reshape(Bs, Ls, D_MODEL),
            jnp.stack(st_p["b_conv"]), jnp.stack(st_p["b_h"]),
            p_c_k, p_c_v, p_c_logf,
            p_mem_k.reshape(DEPTH, Bp, N_MEM, M_HEADS, M_HEAD_DIM),
            p_mem_v.reshape(DEPTH, Bp, N_MEM, M_HEADS, M_HEAD_DIM),
            jnp.stack(st_s["a_v"]), jnp.stack(st_s["b_conv"]), jnp.stack(st_s["b_h"]),
            jnp.stack(st_s["c_k"]), jnp.stack(st_s["c_v"]), jnp.stack(st_s["c_logf"]))
```

```python
import functools

import numpy as np
import jax
import jax.numpy as jnp
from jax import lax
from jax.experimental import pallas as pl
from jax.experimental.pallas import tpu as pltpu

F32 = jnp.float32
BF16 = jnp.bfloat16
HIGHEST = lax.Precision.HIGHEST

D_MODEL = 1024
DEPTH = 4
N_ODD = DEPTH // 2
A_WIDTH = 512
A_GROUPS = 4
A_GROUP_DIM = 128
A_CHUNK = 128
B_WIDTH = 512
B_BLOCKS = 8
B_BLOCK_DIM = 64
B_CONV = 4
B_C = 8.0
C_HEADS = 16
C_HEAD_DIM = 64
C_SCALE = C_HEAD_DIM ** -0.5
N_MEM = 256
M_HEADS = 4
M_HEAD_DIM = 128
M_WIDTH = 512
M_SCALE = M_HEAD_DIM ** -0.5
N_EXPERTS = 32
TOP_K = 4
D_EXPERT = 1024
SWIGLU_LIMIT = 7.0
SWIGLU_ALPHA = 1.702
DN_ALPHA = (2 * DEPTH) ** 0.25
LN_EPS = 1e-5

LANES = 128
SUBLANES = 8
VMEM_LIMIT = 56 * 1024 * 1024
NEG_INF = float("-inf")

MOE_TM = 256
ROW_TM = 512
FOX_TM = 256
ATT_T = 256
DISP_TM = 256

SLOT = LANES
QA_W = C_HEADS * SLOT
N_PIECE = 3
ONE_LANE = N_PIECE * C_HEADS


def _cparams(sem):
    return pltpu.CompilerParams(dimension_semantics=sem, vmem_limit_bytes=VMEM_LIMIT)


def _dot(a, b):
    return jnp.dot(a, b, preferred_element_type=F32)


def _dot_nt(a, b):
    return lax.dot_general(a, b, (((1,), (1,)), ((), ())), preferred_element_type=F32)


def _gelu(x):
    return 0.5 * x * (1.0 + jnp.tanh(0.7978845608028654 * (x + 0.044715 * (x * x * x))))


def _sigmoid(x):
    return 1.0 / (1.0 + jnp.exp(-x))


def _log_sigmoid(x):
    return jnp.minimum(x, 0.0) - jnp.log1p(jnp.exp(-jnp.abs(x)))


def _layer_norm(y, g, b):
    mu = jnp.mean(y, axis=-1, keepdims=True)
    d = y - mu
    var = jnp.mean(d * d, axis=-1, keepdims=True)
    return d * lax.rsqrt(var + LN_EPS) * g + b


def _full(shape):
    n = len(shape)
    return pl.BlockSpec(shape, lambda *_: (0,) * n)


def _any():
    return pl.BlockSpec(memory_space=pl.ANY)


def _call_inplace(kernel, **kw):
    return pl.pallas_call(kernel, input_output_aliases={0: 0}, **kw)


def _proj_kernel(x_ref, w_ref, *o_refs):
    xb = x_ref[...].astype(BF16)
    n = o_refs[0].shape[-1]
    for j, o_ref in enumerate(o_refs):
        o_ref[...] = _dot(xb, w_ref[:, j * n:(j + 1) * n])


def _proj(x, w_bf16, n_out=1, tm=ROW_TM):
    T, K = x.shape
    N = w_bf16.shape[1]
    n = N // n_out
    tm = min(tm, T)
    return pl.pallas_call(
        _proj_kernel,
        grid=(T // tm,),
        in_specs=[pl.BlockSpec((tm, K), lambda i: (i, 0)), _full((K, N))],
        out_specs=[pl.BlockSpec((tm, n), lambda i: (i, 0)) for _ in range(n_out)],
        out_shape=[jax.ShapeDtypeStruct((T, n), F32) for _ in range(n_out)],
        compiler_params=_cparams(("parallel",)),
        name="proj",
    )(x, w_bf16)


def _ab_kernel(x_ref, conv0_ref, h0_ref, w_in_ref, lng_ref, lnb_ref, ws_ref, bst_ref,
               cw_ref, cb_ref, wa_ref, ba_ref, wx_ref, bx_ref, lam_ref, w_out_ref,
               g1_ref, b1_ref, x1_ref, *rest, tl, emit_v):
    if emit_v:
        v_ref, buf_ref, hl_ref, xp_s, h_s = rest
    else:
        buf_ref, hl_ref, xp_s, h_s = rest
        v_ref = None
    l = pl.program_id(1)

    @pl.when(l == 0)
    def _():
        xp_s[0:SUBLANES, :] = conv0_ref[...]
        h_s[...] = h0_ref[...]

    x = x_ref[...]
    proj = _dot(x.astype(BF16), w_in_ref[...])
    u = _gelu(proj[:, 0:A_WIDTH])
    v = _gelu(proj[:, A_WIDTH:2 * A_WIDTH])
    y_br = proj[:, 2 * A_WIDTH:2 * A_WIDTH + B_WIDTH]
    x_br = proj[:, 2 * A_WIDTH + B_WIDTH:]

    row = lax.broadcasted_iota(jnp.int32, (tl, tl), 0)
    col = lax.broadcasted_iota(jnp.int32, (tl, tl), 1)
    causal = col <= row
    out = DN_ALPHA * x
    for g in range(A_GROUPS):
        lo, hi = g * A_GROUP_DIM, (g + 1) * A_GROUP_DIM
        vn = _layer_norm(v[:, lo:hi], lng_ref[:, lo:hi], lnb_ref[:, lo:hi])
        if emit_v:
            v_ref[:, lo:hi] = vn
        wg = jnp.where(causal, ws_ref[g, 0:tl, 0:tl], 0.0).astype(BF16)
        z = _dot(wg, vn.astype(BF16)) + bst_ref[0:tl, g:g + 1]
        a_out = u[:, lo:hi] * z
        out = out + _dot(a_out.astype(BF16), w_out_ref[lo:hi, :])

    xp_s[SUBLANES:SUBLANES + tl, :] = x_br
    xc = (cb_ref[...] + cw_ref[3:4, :] * x_br
          + cw_ref[2:3, :] * xp_s[SUBLANES - 1:SUBLANES - 1 + tl, :]
          + cw_ref[1:2, :] * xp_s[SUBLANES - 2:SUBLANES - 2 + tl, :]
          + cw_ref[0:1, :] * xp_s[SUBLANES - 3:SUBLANES - 3 + tl, :])
    tail = xp_s[tl:tl + SUBLANES, :]
    buf_ref[...] = tail
    xp_s[0:SUBLANES, :] = tail

    xcb = xc.astype(BF16)
    r = _sigmoid(_dot(xcb, wa_ref[...]) + ba_ref[...])
    i = _sigmoid(_dot(xcb, wx_ref[...]) + bx_ref[...])
    lam = lam_ref[...]
    softplus_neg_lam = jnp.maximum(-lam, 0.0) + jnp.log1p(jnp.exp(-jnp.abs(lam)))
    log_a = -B_C * r * softplus_neg_lam
    a = jnp.exp(log_a)
    one_minus_a2 = -jnp.tanh(log_a) * (a * a + 1.0)
    bb = (xc * i) * jnp.sqrt(one_minus_a2)

    rows = lax.broadcasted_iota(jnp.int32, (tl, B_WIDTH), 0)
    k = 1
    while k < tl:
        a_sh = pltpu.roll(a, k, 0)
        b_sh = pltpu.roll(bb, k, 0)
        m = rows >= k
        bb = jnp.where(m, a * b_sh + bb, bb)
        a = jnp.where(m, a * a_sh, a)
        k *= 2
    h = a * h_s[...] + bb
    h_last = h[tl - 1:tl, :]
    h_s[...] = h_last
    hl_ref[...] = h_last
    b_out = _gelu(y_br) * h
    out = out + _dot(b_out.astype(BF16), w_out_ref[A_WIDTH:, :])
    x1_ref[...] = _layer_norm(out, g1_ref[...], b1_ref[...])


def _ab_layer(x_all, row_off, B, L, conv0, h0, w, emit_v):
    T = x_all.shape[0]
    tl = min(L, A_CHUNK)
    nl = L // tl
    off = row_off // tl
    rows = lambda n: pl.BlockSpec((tl, n), lambda b, l: (off + b * nl + l, 0))
    local = lambda n: pl.BlockSpec((tl, n), lambda b, l: (b * nl + l, 0))
    st_spec = lambda r: pl.BlockSpec((None, r, B_WIDTH), lambda b, l: (b, 0, 0))
    weights = [w["w_in"], w["ln_g"], w["ln_b"], w["w_s"], w["b_st"], w["conv_w"], w["conv_b"],
               w["wa"], w["ba"], w["wx"], w["bx"], w["lam"], w["w_out"], w["g1"], w["b1"]]
    in_specs = [rows(D_MODEL), st_spec(SUBLANES), st_spec(1)] + [_full(a.shape) for a in weights]
    out_specs = [rows(D_MODEL)] + ([local(A_WIDTH)] if emit_v else []) + [st_spec(SUBLANES), st_spec(1)]
    out_shape = ([jax.ShapeDtypeStruct((T, D_MODEL), F32)]
                 + ([jax.ShapeDtypeStruct((B * L, A_WIDTH), F32)] if emit_v else [])
                 + [jax.ShapeDtypeStruct((B, SUBLANES, B_WIDTH), F32),
                    jax.ShapeDtypeStruct((B, 1, B_WIDTH), F32)])
    call = _call_inplace(
        functools.partial(_ab_kernel, tl=tl, emit_v=emit_v),
        grid=(B, nl), in_specs=in_specs, out_specs=out_specs, out_shape=out_shape,
        scratch_shapes=[pltpu.VMEM((tl + SUBLANES, B_WIDTH), F32), pltpu.VMEM((1, B_WIDTH), F32)],
        compiler_params=_cparams(("parallel", "arbitrary")), name="ab_mixer")
    return call(x_all, conv0, h0, *weights)


def _out_ln_kernel(x_ref, o_ref, w_ref, g_ref, b_ref, y_ref):
    mix = _dot(o_ref[...].astype(BF16), w_ref[...])
    y_ref[...] = _layer_norm(DN_ALPHA * x_ref[...] + mix, g_ref[...], b_ref[...])


def _out_ln(x_all, row_off, o, w_bf16, g, b, tm=ROW_TM):
    T = x_all.shape[0]
    n, K = o.shape
    tm = min(tm, n)
    off = row_off // tm
    rows = pl.BlockSpec((tm, D_MODEL), lambda i: (off + i, 0))
    in_specs = [rows, pl.BlockSpec((tm, K), lambda i: (i, 0)), _full(w_bf16.shape), _full(g.shape), _full(b.shape)]
    call = _call_inplace(
        _out_ln_kernel,
        grid=(n // tm,), in_specs=in_specs, out_specs=rows,
        out_shape=jax.ShapeDtypeStruct((T, D_MODEL), F32),
        compiler_params=_cparams(("parallel",)), name="out_ln")
    return call(x_all, o, w_bf16, g, b)


def _fox_pre_p_kernel(x_ref, wkt_ref, wvt_ref, wqa_ref, wka_ref, wf_ref, bf_ref, pq_ref, pk_ref,
                      kt_ref, vt_ref, lft_ref, qa_ref, ka_ref, carry_s, *, tm):
    @pl.when(pl.program_id(1) == 0)
    def _():
        carry_s[...] = jnp.zeros(carry_s.shape, F32)

    x = x_ref[...]
    xb = x.astype(BF16)
    kt_ref[...] = _dot_nt(wkt_ref[...], xb)
    vt_ref[...] = _dot_nt(wvt_ref[...], xb)

    x_lo = (x - xb.astype(F32)).astype(BF16)
    wf = wf_ref[...]
    wf_hi = wf.astype(BF16)
    wf_lo = (wf - wf_hi.astype(F32)).astype(BF16)
    f = _dot(xb, wf_hi) + _dot(x_lo, wf_hi) + _dot(xb, wf_lo) + bf_ref[...]
    lane = lax.broadcasted_iota(jnp.int32, (tm, LANES), 1)
    logf = jnp.where(lane < C_HEADS, _log_sigmoid(f), 0.0)
    lft_ref[...] = logf.T[0:C_HEADS, :]

    c = logf
    rows = lax.broadcasted_iota(jnp.int32, (tm, LANES), 0)
    k = 1
    while k < tm:
        c = c + jnp.where(rows >= k, pltpu.roll(c, k, 0), 0.0)
        k *= 2
    c = c + carry_s[...]
    carry_s[...] = c[tm - 1:tm, :]

    hi = c.astype(BF16).astype(F32)
    r1 = c - hi
    mid = r1.astype(BF16).astype(F32)
    lo = (r1 - mid).astype(BF16).astype(F32)
    pieces = (hi + pltpu.roll(mid, C_HEADS, 1) + pltpu.roll(lo, 2 * C_HEADS, 1)
              + jnp.where(lane == ONE_LANE, 1.0, 0.0)).astype(BF16)
    qa_ref[...] = (_dot(xb, wqa_ref[...]) + _dot(pieces, pq_ref[...])).astype(BF16)
    ka_ref[...] = (_dot(xb, wka_ref[...]) + _dot(pieces, pk_ref[...])).astype(BF16)


def _piece_placement():
    pq = np.zeros((LANES, QA_W), np.float32)
    pk = np.zeros((LANES, QA_W), np.float32)
    for h in range(C_HEADS):
        base = h * SLOT + C_HEAD_DIM
        for p in range(N_PIECE):
            pq[p * C_HEADS + h, base + p] = 1.0
            pq[ONE_LANE, base + N_PIECE + p] = 1.0
            pk[ONE_LANE, base + p] = 1.0
            pk[p * C_HEADS + h, base + N_PIECE + p] = -1.0
    return jnp.asarray(pq, BF16), jnp.asarray(pk, BF16)


def _slot_weights(w, scale):
    w = (w * scale).reshape(D_MODEL, C_HEADS, C_HEAD_DIM)
    w = jnp.pad(w, ((0, 0), (0, 0), (0, SLOT - C_HEAD_DIM)))
    return w.reshape(D_MODEL, QA_W).astype(BF16)


def _fox_pre_p(x_all, B, L, w, j, kt_prev, vt_prev, lft_prev, tm=FOX_TM):
    nl = L // tm
    weights = [w["wkt"], w["wvt"], w["wqa"], w["wka"], w["w_f"], w["b_f"], w["pq"], w["pk"]]
    in_specs = [pl.BlockSpec((tm, D_MODEL), lambda b, l: (b * nl + l, 0))] + [_full(a.shape) for a in weights]
    t_spec = lambda n: pl.BlockSpec((None, None, n, tm), lambda b, l: (j, b, 0, l))
    a_spec = pl.BlockSpec((tm, QA_W), lambda b, l: (b * nl + l, 0))
    out_specs = [t_spec(D_MODEL), t_spec(D_MODEL), t_spec(C_HEADS), a_spec, a_spec]
    out_shape = [jax.ShapeDtypeStruct((N_ODD, B, D_MODEL, L), F32), jax.ShapeDtypeStruct((N_ODD, B, D_MODEL, L), F32),
                 jax.ShapeDtypeStruct((N_ODD, B, C_HEADS, L), F32),
                 jax.ShapeDtypeStruct((B * L, QA_W), BF16), jax.ShapeDtypeStruct((B * L, QA_W), BF16)]
    kern = functools.partial(_fox_pre_p_kernel, tm=tm)
    kw = dict(grid=(B, nl), out_specs=out_specs, out_shape=out_shape,
              scratch_shapes=[pltpu.VMEM((1, LANES), F32)],
              compiler_params=_cparams(("parallel", "arbitrary")), name="fox_pre_p")
    n_in = len(in_specs)

    def with_prev(*refs):
        return kern(*refs[:n_in], *refs[n_in + 3:])
    return pl.pallas_call(with_prev, in_specs=in_specs + [_any()] * 3,
                          input_output_aliases={n_in: 0, n_in + 1: 1, n_in + 2: 2}, **kw)(
        x_all, *weights, kt_prev, vt_prev, lft_prev)


def _fox_attn_p_kernel(qa_ref, ka_ref, vt_ref, o_ref, acc_s, *, t):
    qi = pl.program_id(2)
    qas = [qa_ref[:, hh * SLOT:(hh + 1) * SLOT] for hh in range(2)]
    acc_s[...] = jnp.zeros(acc_s.shape, F32)
    krow = lax.broadcasted_iota(jnp.int32, (t, t), 0)
    qcol = lax.broadcasted_iota(jnp.int32, (t, t), 1)

    def scores(off):
        return tuple(_dot_nt(ka_ref[pl.ds(off, t), hh * SLOT:(hh + 1) * SLOT], qas[hh]) for hh in range(2))

    def consume(off, ss, carry, masked):
        new = []
        for hh in range(2):
            m_old, l_old = carry[2 * hh], carry[2 * hh + 1]
            s = ss[hh]
            if masked:
                s = jnp.where(krow <= qcol, s, NEG_INF)
            m_new = jnp.maximum(m_old, jnp.max(s, axis=0, keepdims=True))
            p = jnp.exp(s - m_new)
            alpha = jnp.exp(m_old - m_new)
            l_new = alpha * l_old + jnp.sum(p, axis=0, keepdims=True)
            vc = vt_ref[hh * C_HEAD_DIM:(hh + 1) * C_HEAD_DIM, pl.ds(off, t)].astype(BF16)
            acc_s[hh] = alpha * acc_s[hh] + _dot(vc, p.astype(BF16))
            new += [m_new, l_new]
        return tuple(new)

    def body(jj, c):
        nxt = scores(pl.multiple_of((jj + 1) * t, t))
        return consume(pl.multiple_of(jj * t, t), c[4:], c[:4], False) + nxt

    init = (jnp.full((1, t), NEG_INF, F32), jnp.zeros((1, t), F32)) * 2
    c = lax.fori_loop(0, qi, body, init + scores(0))
    carry = consume(pl.multiple_of(qi * t, t), c[4:], c[:4], True)
    o_t = jnp.concatenate([acc_s[0] / carry[1], acc_s[1] / carry[3]], axis=0)
    o_ref[...] = o_t.T


def _fox_attn_p(qa, ka, vt, j, B, L, t=ATT_T):
    nq = L // t
    n_hp = C_HEADS // 2
    return pl.pallas_call(
        functools.partial(_fox_attn_p_kernel, t=t),
        grid=(B, n_hp, nq),
        in_specs=[pl.BlockSpec((t, 2 * SLOT), lambda b, h, i: (b * nq + i, h)),
                  pl.BlockSpec((L, 2 * SLOT), lambda b, h, i: (b, h)),
                  pl.BlockSpec((None, None, 2 * C_HEAD_DIM, L), lambda b, h, i: (j, b, h, 0))],
        out_specs=pl.BlockSpec((t, LANES), lambda b, h, i: (b * nq + i, h)),
        out_shape=jax.ShapeDtypeStruct((B * L, D_MODEL), F32),
        scratch_shapes=[pltpu.VMEM((2, C_HEAD_DIM, t), F32)],
        compiler_params=_cparams(("parallel", "parallel", "arbitrary")),
        name="fox_attn_p",
    )(qa, ka, vt)


def _fox_pre_s_kernel(x_ref, w_ref, wf_ref, bf_ref, q_ref, k_ref, v_ref, lf_ref):
    x = x_ref[...]
    xb = x.astype(BF16)
    q_ref[...] = _dot(xb, w_ref[:, 0:D_MODEL])
    k_ref[...] = _dot(xb, w_ref[:, D_MODEL:2 * D_MODEL])
    v_ref[...] = _dot(xb, w_ref[:, 2 * D_MODEL:3 * D_MODEL])
    f = jnp.dot(x, wf_ref[...], preferred_element_type=F32, precision=HIGHEST) + bf_ref[...]
    lf_ref[...] = _log_sigmoid(f)[:, 0:C_HEADS]


def _fox_pre_s(x_all, row_off, n, w_qkv, w_f, b_f, tm=ROW_TM):
    tm = min(tm, n)
    off = row_off // tm
    row = lambda c: pl.BlockSpec((tm, c), lambda i: (i, 0))
    return pl.pallas_call(
        _fox_pre_s_kernel,
        grid=(n // tm,),
        in_specs=[pl.BlockSpec((tm, D_MODEL), lambda i: (off + i, 0)),
                  _full(w_qkv.shape), _full(w_f.shape), _full(b_f.shape)],
        out_specs=[row(D_MODEL), row(D_MODEL), row(D_MODEL), row(C_HEADS)],
        out_shape=[jax.ShapeDtypeStruct((n, D_MODEL), F32)] * 3 + [jax.ShapeDtypeStruct((n, C_HEADS), F32)],
        compiler_params=_cparams(("parallel",)),
        name="fox_pre_s",
    )(x_all, w_qkv, w_f, b_f)


def _cumsum_kernel(x_ref, o_ref):
    x = x_ref[...]
    n = x.shape[-1]
    lane = lax.broadcasted_iota(jnp.int32, x.shape, 1)
    k = 1
    while k < n:
        x = x + jnp.where(lane >= k, pltpu.roll(x, k, 1), 0.0)
        k *= 2
    o_ref[...] = x


def _cumsum_lanes(x):
    B, H, Lp = x.shape
    spec = pl.BlockSpec((None, H, Lp), lambda b: (b, 0, 0))
    return pl.pallas_call(
        _cumsum_kernel, grid=(B,), in_specs=[spec], out_specs=spec,
        out_shape=jax.ShapeDtypeStruct(x.shape, F32),
        compiler_params=_cparams(("parallel",)), name="logf_cumsum",
    )(x)


def _fox_attn_s_kernel(q_ref, kn_ref, vn_ref, kt_ref, vt_ref, cq_ref, ck_ref, o_ref, *, L, P):
    row = lax.broadcasted_iota(jnp.int32, (2 * L, LANES), 0)
    lane = lax.broadcasted_iota(jnp.int32, (2 * L, LANES), 1)
    q = q_ref[...] * C_SCALE
    q2 = jnp.concatenate([q, q], axis=0)
    qb = jnp.where((row < L) == (lane < C_HEAD_DIM), q2, 0.0).astype(BF16)
    s_p = _dot(qb, kt_ref[...].astype(BF16))
    s_n = _dot_nt(qb, kn_ref[...].astype(BF16))
    cq2 = jnp.concatenate([cq_ref[0], cq_ref[1]], axis=0)
    first_p = lax.broadcasted_iota(jnp.int32, (2 * L, P), 0) < L
    s_p = s_p + (cq2 - jnp.where(first_p, ck_ref[0:1, 0:P], ck_ref[1:2, 0:P]))
    rn = lax.broadcasted_iota(jnp.int32, (2 * L, L), 0)
    cn = lax.broadcasted_iota(jnp.int32, (2 * L, L), 1)
    s_n = s_n + (cq2 - jnp.where(rn < L, ck_ref[0:1, P:P + L], ck_ref[1:2, P:P + L]))
    s_n = jnp.where(cn <= jnp.where(rn < L, rn, rn - L), s_n, NEG_INF)
    m = jnp.maximum(jnp.max(s_p, axis=-1, keepdims=True), jnp.max(s_n, axis=-1, keepdims=True))
    p_p = jnp.exp(s_p - m)
    p_n = jnp.exp(s_n - m)
    den = jnp.sum(p_p, axis=-1, keepdims=True) + jnp.sum(p_n, axis=-1, keepdims=True)
    o2 = (_dot_nt(p_p.astype(BF16), vt_ref[...].astype(BF16))
          + _dot(p_n.astype(BF16), vn_ref[...].astype(BF16))) / den
    first_lanes = lax.broadcasted_iota(jnp.int32, (L, LANES), 1) < C_HEAD_DIM
    o_ref[...] = jnp.where(first_lanes, o2[0:L], o2[L:2 * L])


def _fox_attn_s(q, k, v, kt_cache, vt_cache, j, cq, ck, B, L):
    P = kt_cache.shape[-1]
    n_hp = C_HEADS // 2
    row = pl.BlockSpec((L, LANES), lambda b, h: (b, h))
    cache = pl.BlockSpec((None, None, 2 * C_HEAD_DIM, P), lambda b, h: (j, b, h, 0))
    return pl.pallas_call(
        functools.partial(_fox_attn_s_kernel, L=L, P=P),
        grid=(B, n_hp),
        in_specs=[row, row, row, cache, cache,
                  pl.BlockSpec((None, None, 2, L, 1), lambda b, h: (b, h, 0, 0, 0)),
                  pl.BlockSpec((None, None, 2, ck.shape[-1]), lambda b, h: (b, h, 0, 0))],
        out_specs=row,
        out_shape=jax.ShapeDtypeStruct((B * L, D_MODEL), F32),
        compiler_params=_cparams(("parallel", "parallel")),
        name="fox_attn_s",
    )(q, k, v, kt_cache, vt_cache, cq, ck)


def _mem_kernel(x_ref, mk_ref, mv_ref, wq_ref, wo_ref, g_ref, b_ref, y_ref):
    x = x_ref[...]
    q = _dot(x.astype(BF16), wq_ref[...]).astype(BF16)
    mv_t = mv_ref[...].T.astype(BF16)
    o_t = []
    for h in range(M_HEADS):
        lo, hi = h * M_HEAD_DIM, (h + 1) * M_HEAD_DIM
        s = _dot_nt(mk_ref[:, lo:hi].astype(BF16), q[:, lo:hi]) * M_SCALE
        p = jnp.exp(s - jnp.max(s, axis=0, keepdims=True))
        den = jnp.sum(p, axis=0, keepdims=True)
        o_t.append(_dot(mv_t[lo:hi, :], p.astype(BF16)) / den)
    o = jnp.concatenate(o_t, axis=0).T
    out = DN_ALPHA * x + _dot(o.astype(BF16), wo_ref[...])
    y_ref[...] = _layer_norm(out, g_ref[...], b_ref[...])


def _mem_layer(x_all, row_off, B, L, mk, mv, layer, w_q, w_o, g, b, tl=256):
    T = x_all.shape[0]
    tl = min(tl, L)
    nl = L // tl
    off = row_off // tl
    rows = pl.BlockSpec((tl, D_MODEL), lambda bb, l: (off + bb * nl + l, 0))
    mem = pl.BlockSpec((None, None, N_MEM, M_WIDTH), lambda bb, l: (layer, bb, 0, 0))
    in_specs = [rows, mem, mem, _full(w_q.shape), _full(w_o.shape), _full(g.shape), _full(b.shape)]
    call = _call_inplace(
        _mem_kernel,
        grid=(B, nl), in_specs=in_specs, out_specs=rows,
        out_shape=jax.ShapeDtypeStruct((T, D_MODEL), F32),
        compiler_params=_cparams(("parallel", "parallel")), name="mem_attend")
    return call(x_all, mk, mv, w_q, w_o, g, b)


def _router_kernel(x_ref, wr_ref, br_ref, mi_ref, mw_ref, cnt_ref, carry_s, *, tm):
    @pl.when(pl.program_id(0) == 0)
    def _():
        carry_s[...] = jnp.zeros(carry_s.shape, F32)

    x = x_ref[...]
    logits = jnp.dot(x, wr_ref[...], preferred_element_type=F32, precision=HIGHEST) + br_ref[...]
    lane = lax.broadcasted_iota(jnp.int32, (tm, LANES), 1)
    work = jnp.where(lane < N_EXPERTS, logits, NEG_INF)
    vals, idxs = [], []
    for _ in range(TOP_K):
        mx = jnp.max(work, axis=-1, keepdims=True)
        idx = jnp.min(jnp.where(work == mx, lane, LANES), axis=-1, keepdims=True)
        vals.append(mx)
        idxs.append(idx)
        work = jnp.where(lane == idx, NEG_INF, work)
    es = [jnp.exp(vv - vals[0]) for vv in vals]
    den = es[0] + es[1] + es[2] + es[3]
    sel = jnp.zeros((tm, LANES), F32)
    for idx in idxs:
        sel = jnp.where(lane == idx, 1.0, sel)
    rr = lax.broadcasted_iota(jnp.int32, (tm, tm), 0)
    cc = lax.broadcasted_iota(jnp.int32, (tm, tm), 1)
    strict = jnp.where(cc < rr, 1.0, 0.0).astype(BF16)
    rank = carry_s[...] + _dot(strict, sel.astype(BF16))
    mi = jnp.zeros((tm, LANES), jnp.int32)
    mw = jnp.zeros((tm, LANES), F32)
    for kk in range(TOP_K):
        rk = jnp.sum(jnp.where(lane == idxs[kk], rank, 0.0), axis=-1, keepdims=True)
        mi = jnp.where(lane == kk, idxs[kk], mi)
        mi = jnp.where(lane == TOP_K + kk, rk.astype(jnp.int32), mi)
        mw = jnp.where(lane == kk, es[kk] / den, mw)
    mi_ref[...] = mi
    mw_ref[...] = mw
    total = carry_s[...] + jnp.sum(sel, axis=0, keepdims=True)
    carry_s[...] = total
    cnt_ref[...] = total


def _router(x, w_r, b_r, tm=ROW_TM):
    T = x.shape[0]
    tm = min(tm, T)
    row = lambda: pl.BlockSpec((tm, LANES), lambda i: (i, 0))
    return pl.pallas_call(
        functools.partial(_router_kernel, tm=tm),
        grid=(T // tm,),
        in_specs=[pl.BlockSpec((tm, D_MODEL), lambda i: (i, 0)), _full(w_r.shape), _full(b_r.shape)],
        out_specs=[row(), row(), _full((1, LANES))],
        out_shape=[jax.ShapeDtypeStruct((T, LANES), jnp.int32), jax.ShapeDtypeStruct((T, LANES), F32),
                   jax.ShapeDtypeStruct((1, LANES), F32)],
        scratch_shapes=[pltpu.VMEM((1, LANES), F32)],
        compiler_params=_cparams(("arbitrary",)),
        name="router",
    )(x, w_r, b_r)


def _invert_kernel(gs_ref, cnt_ref, tp_ref, pos_hbm, perm_ref, chunk_s, sem, *, n_rows, n_real, tm, chunk):
    step = pl.program_id(0)

    @pl.when(step == 0)
    def _():
        def zero(r, c):
            perm_ref[r] = 0
            return c
        lax.fori_loop(0, n_rows, zero, 0, unroll=16)

        def prime(r, c):
            perm_ref[n_rows + r] = n_rows + r
            return c
        lax.fori_loop(0, tm, prime, 0, unroll=16)

        def pads(e, dump):
            lo = gs_ref[e] + cnt_ref[e]
            hi = gs_ref[e] + tp_ref[e] * tm

            def one(r, d):
                perm_ref[r] = d
                return d + 1
            return lax.fori_loop(lo, hi, one, dump)
        lax.fori_loop(0, N_EXPERTS, pads, n_real)

    base = pl.multiple_of(step * chunk, chunk)
    copy = pltpu.make_async_copy(pos_hbm.at[pl.ds(base, chunk)], chunk_s, sem)
    copy.start()
    copy.wait()

    def assign(a, c):
        perm_ref[chunk_s[a]] = base + a
        return c
    lax.fori_loop(0, chunk, assign, 0, unroll=8)


def _invert(pos_flat, group_start, cnt, tiles_per, n_rows, tm=MOE_TM, chunk=8192):
    n_real = pos_flat.shape[0]
    chunk = min(chunk, n_real)
    assert n_real % chunk == 0
    return pl.pallas_call(
        functools.partial(_invert_kernel, n_rows=n_rows, n_real=n_real, tm=tm, chunk=chunk),
        grid_spec=pltpu.PrefetchScalarGridSpec(
            num_scalar_prefetch=3,
            grid=(n_real // chunk,),
            in_specs=[_any()],
            out_specs=pl.BlockSpec(memory_space=pltpu.SMEM),
            scratch_shapes=[pltpu.SMEM((chunk,), jnp.int32), pltpu.SemaphoreType.DMA(())],
        ),
        out_shape=jax.ShapeDtypeStruct((n_rows + tm,), jnp.int32),
        compiler_params=_cparams(("arbitrary",)),
        name="moe_invert",
    )(group_start, cnt, tiles_per, pos_flat)


GU_CHUNKS = 8
DN_CHUNKS = 4


def _gmm_kernel(te_ref, nu_ref, perm_ref, x_hbm, wgu_ref, bgu_ref, wdn_ref, bdn_ref, y4_in, y4_ref,
                xbuf, ybuf, gsem, ssem, wgu_s, wdn_s, *, tm, n_rows, t_max):
    del y4_in
    i = pl.program_id(0)
    n_used = nu_ref[0]
    slot = lax.rem(i, 2)
    nslot = 1 - slot

    def gather_start(base, r, s):
        tok = jnp.minimum(lax.shift_right_logical(perm_ref[base + r], 2), t_max)
        pltpu.make_async_copy(x_hbm.at[pl.ds(tok, 1)], xbuf.at[s, pl.ds(r, 1)], gsem.at[s]).start()

    def scatter_start(base, r, s):
        pltpu.make_async_copy(ybuf.at[s, pl.ds(r, 1)], y4_ref.at[pl.ds(perm_ref[base + r], 1)], ssem.at[s]).start()

    def gather_wait(s):
        pltpu.make_async_copy(x_hbm.at[pl.ds(0, tm)], xbuf.at[s], gsem.at[s]).wait()

    def scatter_wait(s):
        pltpu.make_async_copy(ybuf.at[s], y4_ref.at[pl.ds(0, tm)], ssem.at[s]).wait()

    @pl.when(i == 0)
    def _():
        ybuf[...] = jnp.zeros(ybuf.shape, F32)
        pltpu.make_async_copy(ybuf.at[0], y4_ref.at[pl.ds(n_rows + tm, tm)], ssem.at[0]).start()

        def first(r, c):
            gather_start(0, r, 0)
            return c
        lax.fori_loop(0, tm, first, 0)

    @pl.when((i == 0) | (te_ref[i] != te_ref[jnp.maximum(i - 1, 0)]))
    def _():
        def cast_rows(c, carry):
            r = pl.multiple_of(c * LANES, LANES)
            wgu_s[pl.ds(r, LANES), :] = wgu_ref[pl.ds(r, LANES), :].astype(BF16)
            wdn_s[pl.ds(r, LANES), :] = wdn_ref[pl.ds(r, LANES), :].astype(BF16)
            return carry
        lax.fori_loop(0, D_MODEL // LANES, cast_rows, 0)

    @pl.when(i < n_used)
    def _():
        gather_wait(slot)
        scatter_wait(slot)
        xb = xbuf[slot].astype(BF16)
        nxt = (i + 1) * tm
        prv = jnp.where(i == 0, n_rows, (i - 1) * tm)
        moved = 0
        per_gu = tm // (2 * GU_CHUNKS)
        per_dn = tm // (2 * DN_CHUNKS)
        cw = 2 * D_EXPERT // GU_CHUNKS
        gu = []
        for c in range(GU_CHUNKS):
            gu.append(_dot(xb, wgu_s[:, c * cw:(c + 1) * cw]) + bgu_ref[:, c * cw:(c + 1) * cw])
            for r in range(moved, moved + per_gu):
                gather_start(nxt, r, nslot)
                scatter_start(prv, r, nslot)
            moved += per_gu
        half = GU_CHUNKS // 2
        hs = []
        for c in range(half):
            g = jnp.minimum(gu[c], SWIGLU_LIMIT)
            u = jnp.clip(gu[half + c], -SWIGLU_LIMIT, SWIGLU_LIMIT)
            hs.append(((u + 1.0) * g * _sigmoid(SWIGLU_ALPHA * g)).astype(BF16))
        h = jnp.concatenate(hs, axis=1)
        dw = D_MODEL // DN_CHUNKS
        for c in range(DN_CHUNKS):
            ybuf[slot, :, c * dw:(c + 1) * dw] = _dot(h, wdn_s[:, c * dw:(c + 1) * dw]) + bdn_ref[:, c * dw:(c + 1) * dw]
            for r in range(moved, moved + per_dn):
                gather_start(nxt, r, nslot)
                scatter_start(prv, r, nslot)
            moved += per_dn

    @pl.when(i == n_used)
    def _():
        gather_wait(slot)
        scatter_wait(slot)

        def last(r, c):
            scatter_start((i - 1) * tm, r, nslot)
            return c
        lax.fori_loop(0, tm, last, 0)
        scatter_wait(nslot)


def _gmm(tile_expert, n_used, perm, x, w_gu, b_gu, w_dn, b_dn, y4, layer, tm=MOE_TM):
    n_rows = y4.shape[0] - 2 * tm
    wmap = lambda i, te, nu, pm: (layer, te[i], 0, 0)
    return pl.pallas_call(
        functools.partial(_gmm_kernel, tm=tm, n_rows=n_rows, t_max=x.shape[0] - 1),
        grid_spec=pltpu.PrefetchScalarGridSpec(
            num_scalar_prefetch=3,
            grid=(n_rows // tm + 1,),
            in_specs=[_any(),
                      pl.BlockSpec((None, None, D_MODEL, 2 * D_EXPERT), wmap),
                      pl.BlockSpec((None, None, 1, 2 * D_EXPERT), wmap),
                      pl.BlockSpec((None, None, D_EXPERT, D_MODEL), wmap),
                      pl.BlockSpec((None, None, 1, D_MODEL), wmap),
                      _any()],
            out_specs=_any(),
            scratch_shapes=[pltpu.VMEM((2, tm, D_MODEL), F32), pltpu.VMEM((2, tm, D_MODEL), F32),
                            pltpu.SemaphoreType.DMA((2,)), pltpu.SemaphoreType.DMA((2,)),
                            pltpu.VMEM((D_MODEL, 2 * D_EXPERT), BF16), pltpu.VMEM((D_EXPERT, D_MODEL), BF16)],
        ),
        out_shape=jax.ShapeDtypeStruct(y4.shape, F32),
        input_output_aliases={8: 0},
        compiler_params=_cparams(("arbitrary",)),
        name="moe_gmm",
    )(tile_expert, n_used, perm, x, w_gu, b_gu, w_dn, b_dn, y4)


def _combine_kernel(x_ref, mw_ref, y4_ref, g_ref, b_ref, y_ref):
    out = DN_ALPHA * x_ref[...]
    for kk in range(TOP_K):
        out = out + mw_ref[:, kk:kk + 1] * y4_ref[:, kk * D_MODEL:(kk + 1) * D_MODEL]
    y_ref[...] = _layer_norm(out, g_ref[...], b_ref[...])


def _combine(x, meta_w, y4, g, b, tm=DISP_TM):
    T = x.shape[0]
    tm = min(tm, T)
    y_wide = y4.reshape(y4.shape[0] // TOP_K, TOP_K * D_MODEL)
    rows = pl.BlockSpec((tm, D_MODEL), lambda i: (i, 0))
    return _call_inplace(
        _combine_kernel,
        grid=(T // tm,),
        in_specs=[rows, pl.BlockSpec((tm, LANES), lambda i: (i, 0)),
                  pl.BlockSpec((tm, TOP_K * D_MODEL), lambda i: (i, 0)), _full(g.shape), _full(b.shape)],
        out_specs=rows,
        out_shape=jax.ShapeDtypeStruct((T, D_MODEL), F32),
        compiler_params=_cparams(("parallel",)),
        name="moe_combine",
    )(x, meta_w, y_wide, g, b)


def _moe_layer(x, y4, w, layer, tm=MOE_TM):
    T = x.shape[0]
    n_rows = y4.shape[0] - 2 * tm
    n_tiles = n_rows // tm
    meta_i, meta_w, counts = _router(x, w["w_r"], w["b_r"])
    cnt = counts[0, :N_EXPERTS].astype(jnp.int32)
    tiles_per = (cnt + tm - 1) // tm
    tile_end = jnp.cumsum(tiles_per)
    group_start = (tile_end - tiles_per) * tm
    n_used = tile_end[-1:]
    tile_ids = jnp.arange(n_tiles + 1, dtype=jnp.int32)
    tile_expert = jnp.sum(tile_ids[:, None] >= tile_end[None, :], axis=1).astype(jnp.int32)
    last_expert = jnp.sum(jnp.maximum(n_used - 1, 0) >= tile_end).astype(jnp.int32)
    tile_expert = jnp.where(tile_ids < n_used, tile_expert, last_expert)
    pos = group_start[meta_i[:, :TOP_K]] + meta_i[:, TOP_K:2 * TOP_K]
    pos_flat = pos.reshape(T * TOP_K).astype(jnp.int32)
    perm = _invert(pos_flat, group_start.astype(jnp.int32), cnt, tiles_per.astype(jnp.int32), n_rows, tm)
    y4 = _gmm(tile_expert, n_used.astype(jnp.int32), perm, x, w["w_gu"], w["b_gu"], w["w_dn"], w["b_dn"], y4, layer, tm)
    return _combine(x, meta_w, y4, w["g3"], w["b3"]), y4


def _block_diag(w):
    eye = jnp.eye(B_BLOCKS, dtype=w.dtype)
    return jnp.einsum("hij,hg->higj", w, eye).reshape(B_WIDTH, B_WIDTH)


def _pad_lanes(a, value=0.0):
    return jnp.pad(a, ((0, 0),) * (a.ndim - 1) + ((0, LANES - a.shape[-1]),), constant_values=value)


def kernel(x_prompt, x_sample, state_b_conv, state_b_h, cache_c_k, cache_c_v, cache_c_logf, cache_mem_k, cache_mem_v, mem_prompt, w_in_ab, a_ln_g, a_ln_b, a_w_s, a_b_s, b_conv_w, b_conv_b, b_w_a, b_b_a, b_w_x, b_b_x, b_lambda, w_out_ab, c_w_in, c_b_f, c_w_out, w_mq, w_mk, w_mv, w_mo, w_router, b_router, w_gu, b_gu, w_dn, b_dn, ln1_g, ln1_b, ln2_g, ln2_b, ln3_g, ln3_b):
    Bp, Lp, _ = x_prompt.shape
    Bs, Ls, _ = x_sample.shape
    P = cache_c_k.shape[2]
    Tp, Ts = Bp * Lp, Bs * Ls
    T = Tp + Ts
    row = lambda a: a.reshape(1, -1)

    mem_flat = mem_prompt.reshape(Bp * N_MEM, D_MODEL)
    w_mkv = jnp.concatenate([w_mk, w_mv], axis=-1).astype(BF16)
    p_mem = [_proj(mem_flat, w_mkv[l], n_out=2) for l in range(DEPTH)]
    p_mem_k = jnp.stack([p[0] for p in p_mem]).reshape(DEPTH, Bp, N_MEM, M_WIDTH)
    p_mem_v = jnp.stack([p[1] for p in p_mem]).reshape(DEPTH, Bp, N_MEM, M_WIDTH)
    s_mem_k = cache_mem_k.reshape(DEPTH, Bs, N_MEM, M_WIDTH)
    s_mem_v = cache_mem_v.reshape(DEPTH, Bs, N_MEM, M_WIDTH)

    kt_cache = jnp.transpose(cache_c_k, (0, 1, 3, 4, 2)).reshape(N_ODD, Bs, D_MODEL, P)
    vt_cache = jnp.transpose(cache_c_v, (0, 1, 3, 4, 2)).reshape(N_ODD, Bs, D_MODEL, P)
    lft_cache = jnp.transpose(cache_c_logf, (0, 1, 3, 2))
    pq, pk = _piece_placement()

    b_gu4 = b_gu[:, :, None, :]
    b_dn4 = b_dn[:, :, None, :]
    y4 = jnp.zeros((T * TOP_K + (N_EXPERTS + 2) * MOE_TM, D_MODEL), F32)

    x_all = jnp.concatenate([x_prompt.reshape(Tp, D_MODEL), x_sample.reshape(Ts, D_MODEL)], axis=0)
    kt_p = jnp.zeros((N_ODD, Bp, D_MODEL, Lp), F32)
    vt_p = jnp.zeros((N_ODD, Bp, D_MODEL, Lp), F32)
    lft_p = jnp.zeros((N_ODD, Bp, C_HEADS, Lp), F32)
    st_p = {"b_conv": [], "b_h": []}
    st_s = {"a_v": [], "b_conv": [], "b_h": [], "c_k": [], "c_v": [], "c_logf": []}
    for layer in range(DEPTH):
        j = layer // 2
        g1, b1 = row(ln1_g[layer]), row(ln1_b[layer])
        if layer % 2 == 0:
            w = dict(w_in=w_in_ab[j].astype(BF16), ln_g=row(a_ln_g[j]), ln_b=row(a_ln_b[j]), w_s=a_w_s[j],
                     b_st=a_b_s[j].T, conv_w=b_conv_w[j], conv_b=row(b_conv_b[j]),
                     wa=_block_diag(b_w_a[j]).astype(BF16), ba=row(b_b_a[j]),
                     wx=_block_diag(b_w_x[j]).astype(BF16), bx=row(b_b_x[j]), lam=row(b_lambda[j]),
                     w_out=w_out_ab[j].astype(BF16), g1=g1, b1=b1)
            conv0_p = jnp.zeros((Bp, SUBLANES, B_WIDTH), F32)
            h0_p = jnp.zeros((Bp, 1, B_WIDTH), F32)
            x_all, buf_p, hl_p = _ab_layer(x_all, 0, Bp, Lp, conv0_p, h0_p, w, False)
            conv0_s = jnp.pad(state_b_conv[j], ((0, 0), (SUBLANES - (B_CONV - 1), 0), (0, 0)))
            x_all, v_s, buf_s, hl_s = _ab_layer(x_all, Tp, Bs, Ls, conv0_s, state_b_h[j][:, None, :], w, True)
            st_p["b_conv"].append(buf_p[:, SUBLANES - (B_CONV - 1):])
            st_p["b_h"].append(hl_p[:, 0])
            st_s["a_v"].append(v_s.reshape(Bs, Ls, A_WIDTH))
            st_s["b_conv"].append(buf_s[:, SUBLANES - (B_CONV - 1):])
            st_s["b_h"].append(hl_s[:, 0])
        else:
            wq, wk, wv = (c_w_in[j][:, i * D_MODEL:(i + 1) * D_MODEL] for i in range(3))
            w_f = _pad_lanes(c_w_in[j][:, 3 * D_MODEL:])
            b_f = _pad_lanes(row(c_b_f[j]))
            w_out = c_w_out[j].astype(BF16)
            wp = dict(wkt=wk.T.astype(BF16), wvt=wv.T.astype(BF16), wqa=_slot_weights(wq, C_SCALE),
                      wka=_slot_weights(wk, 1.0), w_f=w_f, b_f=b_f, pq=pq, pk=pk)
            kt_p, vt_p, lft_p, qa, ka = _fox_pre_p(x_all, Bp, Lp, wp, j, kt_p, vt_p, lft_p)
            o_p = _fox_attn_p(qa, ka, vt_p, j, Bp, Lp)
            q_s, k_s, v_s2, lf_s = _fox_pre_s(x_all, Tp, Ts, c_w_in[j][:, :3 * D_MODEL].astype(BF16), w_f, b_f)
            tot = P + Ls
            tot_pad = -(-tot // LANES) * LANES
            lf_t = jnp.concatenate([lft_cache[j], lf_s.reshape(Bs, Ls, C_HEADS).transpose(0, 2, 1),
                                    jnp.zeros((Bs, C_HEADS, tot_pad - tot), F32)], axis=2)
            cum = _cumsum_lanes(lf_t)
            ck = cum.reshape(Bs, C_HEADS // 2, 2, tot_pad)
            cq = cum[:, :, P:P + Ls].reshape(Bs, C_HEADS // 2, 2, Ls, 1)
            o_s = _fox_attn_s(q_s, k_s, v_s2, kt_cache, vt_cache, j, cq, ck, Bs, Ls)
            x_all = _out_ln(x_all, 0, o_p, w_out, g1, b1)
            x_all = _out_ln(x_all, Tp, o_s, w_out, g1, b1)
            st_s["c_k"].append(k_s.reshape(Bs, Ls, C_HEADS, C_HEAD_DIM))
            st_s["c_v"].append(v_s2.reshape(Bs, Ls, C_HEADS, C_HEAD_DIM))
            st_s["c_logf"].append(lf_s.reshape(Bs, Ls, C_HEADS))
        g2, b2 = row(ln2_g[layer]), row(ln2_b[layer])
        wq_m, wo_m = w_mq[layer].astype(BF16), w_mo[layer].astype(BF16)
        x_all = _mem_layer(x_all, 0, Bp, Lp, p_mem_k, p_mem_v, layer, wq_m, wo_m, g2, b2)
        x_all = _mem_layer(x_all, Tp, Bs, Ls, s_mem_k, s_mem_v, layer, wq_m, wo_m, g2, b2)
        wm = dict(w_r=_pad_lanes(w_router[layer]), b_r=_pad_lanes(row(b_router[layer])),
                  w_gu=w_gu, b_gu=b_gu4, w_dn=w_dn, b_dn=b_dn4,
                  g3=row(ln3_g[layer]), b3=row(ln3_b[layer]))
        x_all, y4 = _moe_layer(x_all, y4, wm, layer)

    p_c_k = jnp.transpose(kt_p.reshape(N_ODD, Bp, C_HEADS, C_HEAD_DIM, Lp), (0, 1, 4, 2, 3))
    p_c_v = jnp.transpose(vt_p.reshape(N_ODD, Bp, C_HEADS, C_HEAD_DIM, Lp), (0, 1, 4, 2, 3))
    p_c_logf = jnp.transpose(lft_p, (0, 1, 3, 2))
    return (x_all[:Tp].reshape(Bp, Lp, D_MODEL), x_all[Tp:].reshape(Bs, Ls, D_MODEL),
            jnp.stack(st_p["b_conv"]), jnp.stack(st_p["b_h"]),
            p_c_k, p_c_v, p_c_logf,
            p_mem_k.reshape(DEPTH, Bp, N_MEM, M_HEADS, M_HEAD_DIM),
            p_mem_v.reshape(DEPTH, Bp, N_MEM, M_HEADS, M_HEAD_DIM),
            jnp.stack(st_s["a_v"]), jnp.stack(st_s["b_conv"]), jnp.stack(st_s["b_h"]),
            jnp.stack(st_s["c_k"]), jnp.stack(st_s["c_v"]), jnp.stack(st_s["c_logf"]))
```

```python
import functools

import numpy as np
import jax
import jax.numpy as jnp
from jax import lax
from jax.experimental import pallas as pl
from jax.experimental.pallas import tpu as pltpu

F32 = jnp.float32
BF16 = jnp.bfloat16

D_MODEL = 1024
DEPTH = 4
N_ODD = DEPTH // 2
A_WIDTH = 512
A_GROUPS = 4
A_GROUP_DIM = 128
A_CHUNK = 128
B_WIDTH = 512
B_BLOCKS = 8
B_BLOCK_DIM = 64
B_CONV = 4
B_C = 8.0
C_HEADS = 16
C_HEAD_DIM = 64
C_SCALE = C_HEAD_DIM ** -0.5
N_MEM = 256
M_HEADS = 4
M_HEAD_DIM = 128
M_WIDTH = 512
M_SCALE = M_HEAD_DIM ** -0.5
N_EXPERTS = 32
TOP_K = 4
D_EXPERT = 1024
SWIGLU_LIMIT = 7.0
SWIGLU_ALPHA = 1.702
DN_ALPHA = (2 * DEPTH) ** 0.25
LN_EPS = 1e-5

LANES = 128
SUBLANES = 8
VMEM_LIMIT = 56 * 1024 * 1024
NEG_INF = float("-inf")

MOE_TM = 256
ROW_TM = 512
FOX_TM = 256
ATT_T = 512
ATT_TK = 256
DISP_TM = 256

SLOT = LANES
QA_W = C_HEADS * SLOT
N_PIECE = 3
ONE_LANE = N_PIECE * C_HEADS


def _cparams(sem):
    return pltpu.CompilerParams(dimension_semantics=sem, vmem_limit_bytes=VMEM_LIMIT)


def _dot(a, b):
    return jnp.dot(a, b, preferred_element_type=F32)


def _dot_nt(a, b):
    return lax.dot_general(a, b, (((1,), (1,)), ((), ())), preferred_element_type=F32)


def _dot_x3(x, w):
    x_hi = x.astype(BF16)
    x_lo = (x - x_hi.astype(F32)).astype(BF16)
    w_hi = w.astype(BF16)
    w_lo = (w - w_hi.astype(F32)).astype(BF16)
    return _dot(x_hi, w_hi) + _dot(x_lo, w_hi) + _dot(x_hi, w_lo)


def _gelu(x):
    return 0.5 * x * (1.0 + jnp.tanh(0.7978845608028654 * (x + 0.044715 * (x * x * x))))


def _sigmoid(x):
    return 1.0 / (1.0 + jnp.exp(-x))


def _log_sigmoid(x):
    return jnp.minimum(x, 0.0) - jnp.log1p(jnp.exp(-jnp.abs(x)))


def _layer_norm(y, g, b):
    mu = jnp.mean(y, axis=-1, keepdims=True)
    d = y - mu
    var = jnp.mean(d * d, axis=-1, keepdims=True)
    return d * lax.rsqrt(var + LN_EPS) * g + b


def _full(shape):
    n = len(shape)
    return pl.BlockSpec(shape, lambda *_: (0,) * n)


def _any():
    return pl.BlockSpec(memory_space=pl.ANY)


def _call_inplace(kernel, **kw):
    return pl.pallas_call(kernel, input_output_aliases={0: 0}, **kw)


def _proj_kernel(x_ref, w_ref, *o_refs):
    xb = x_ref[...].astype(BF16)
    n = o_refs[0].shape[-1]
    for j, o_ref in enumerate(o_refs):
        o_ref[...] = _dot(xb, w_ref[:, j * n:(j + 1) * n])


def _proj(x, w_bf16, n_out=1, tm=ROW_TM):
    T, K = x.shape
    N = w_bf16.shape[1]
    n = N // n_out
    tm = min(tm, T)
    return pl.pallas_call(
        _proj_kernel,
        grid=(T // tm,),
        in_specs=[pl.BlockSpec((tm, K), lambda i: (i, 0)), _full((K, N))],
        out_specs=[pl.BlockSpec((tm, n), lambda i: (i, 0)) for _ in range(n_out)],
        out_shape=[jax.ShapeDtypeStruct((T, n), F32) for _ in range(n_out)],
        compiler_params=_cparams(("parallel",)),
        name="proj",
    )(x, w_bf16)


def _ab_kernel(x_ref, conv0_ref, h0_ref, w_in_ref, lng_ref, lnb_ref, ws_ref, bst_ref,
               cw_ref, cb_ref, wa_ref, ba_ref, wx_ref, bx_ref, lam_ref, w_out_ref,
               g1_ref, b1_ref, x1_ref, *rest, tl, emit_v):
    if emit_v:
        v_ref, buf_ref, hl_ref, xp_s, h_s = rest
    else:
        buf_ref, hl_ref, xp_s, h_s = rest
        v_ref = None
    l = pl.program_id(1)

    @pl.when(l == 0)
    def _():
        xp_s[0:SUBLANES, :] = conv0_ref[...]
        h_s[...] = h0_ref[...]

    x = x_ref[...]
    proj = _dot(x.astype(BF16), w_in_ref[...])
    u = _gelu(proj[:, 0:A_WIDTH])
    v = _gelu(proj[:, A_WIDTH:2 * A_WIDTH])
    y_br = proj[:, 2 * A_WIDTH:2 * A_WIDTH + B_WIDTH]
    x_br = proj[:, 2 * A_WIDTH + B_WIDTH:]

    row = lax.broadcasted_iota(jnp.int32, (tl, tl), 0)
    col = lax.broadcasted_iota(jnp.int32, (tl, tl), 1)
    causal = col <= row
    out = DN_ALPHA * x
    for g in range(A_GROUPS):
        lo, hi = g * A_GROUP_DIM, (g + 1) * A_GROUP_DIM
        vn = _layer_norm(v[:, lo:hi], lng_ref[:, lo:hi], lnb_ref[:, lo:hi])
        if emit_v:
            v_ref[:, lo:hi] = vn
        wg = jnp.where(causal, ws_ref[g, 0:tl, 0:tl], 0.0).astype(BF16)
        z = _dot(wg, vn.astype(BF16)) + bst_ref[0:tl, g:g + 1]
        a_out = u[:, lo:hi] * z
        out = out + _dot(a_out.astype(BF16), w_out_ref[lo:hi, :])

    xp_s[SUBLANES:SUBLANES + tl, :] = x_br
    xc = (cb_ref[...] + cw_ref[3:4, :] * x_br
          + cw_ref[2:3, :] * xp_s[SUBLANES - 1:SUBLANES - 1 + tl, :]
          + cw_ref[1:2, :] * xp_s[SUBLANES - 2:SUBLANES - 2 + tl, :]
          + cw_ref[0:1, :] * xp_s[SUBLANES - 3:SUBLANES - 3 + tl, :])
    tail = xp_s[tl:tl + SUBLANES, :]
    buf_ref[...] = tail
    xp_s[0:SUBLANES, :] = tail

    xcb = xc.astype(BF16)
    r = _sigmoid(_dot(xcb, wa_ref[...]) + ba_ref[...])
    i = _sigmoid(_dot(xcb, wx_ref[...]) + bx_ref[...])
    lam = lam_ref[...]
    softplus_neg_lam = jnp.maximum(-lam, 0.0) + jnp.log1p(jnp.exp(-jnp.abs(lam)))
    log_a = -B_C * r * softplus_neg_lam
    a = jnp.exp(log_a)
    one_minus_a2 = -jnp.tanh(log_a) * (a * a + 1.0)
    bb = (xc * i) * jnp.sqrt(one_minus_a2)

    rows = lax.broadcasted_iota(jnp.int32, (tl, B_WIDTH), 0)
    k = 1
    while k < tl:
        a_sh = pltpu.roll(a, k, 0)
        b_sh = pltpu.roll(bb, k, 0)
        m = rows >= k
        bb = jnp.where(m, a * b_sh + bb, bb)
        a = jnp.where(m, a * a_sh, a)
        k *= 2
    h = a * h_s[...] + bb
    h_last = h[tl - 1:tl, :]
    h_s[...] = h_last
    hl_ref[...] = h_last
    b_out = _gelu(y_br) * h
    out = out + _dot(b_out.astype(BF16), w_out_ref[A_WIDTH:, :])
    x1_ref[...] = _layer_norm(out, g1_ref[...], b1_ref[...])


def _ab_layer(x_all, row_off, B, L, conv0, h0, w, emit_v):
    T = x_all.shape[0]
    tl = min(L, A_CHUNK)
    nl = L // tl
    off = row_off // tl
    rows = lambda n: pl.BlockSpec((tl, n), lambda b, l: (off + b * nl + l, 0))
    local = lambda n: pl.BlockSpec((tl, n), lambda b, l: (b * nl + l, 0))
    st_spec = lambda r: pl.BlockSpec((None, r, B_WIDTH), lambda b, l: (b, 0, 0))
    weights = [w["w_in"], w["ln_g"], w["ln_b"], w["w_s"], w["b_st"], w["conv_w"], w["conv_b"],
               w["wa"], w["ba"], w["wx"], w["bx"], w["lam"], w["w_out"], w["g1"], w["b1"]]
    in_specs = [rows(D_MODEL), st_spec(SUBLANES), st_spec(1)] + [_full(a.shape) for a in weights]
    out_specs = [rows(D_MODEL)] + ([local(A_WIDTH)] if emit_v else []) + [st_spec(SUBLANES), st_spec(1)]
    out_shape = ([jax.ShapeDtypeStruct((T, D_MODEL), F32)]
                 + ([jax.ShapeDtypeStruct((B * L, A_WIDTH), F32)] if emit_v else [])
                 + [jax.ShapeDtypeStruct((B, SUBLANES, B_WIDTH), F32),
                    jax.ShapeDtypeStruct((B, 1, B_WIDTH), F32)])
    call = _call_inplace(
        functools.partial(_ab_kernel, tl=tl, emit_v=emit_v),
        grid=(B, nl), in_specs=in_specs, out_specs=out_specs, out_shape=out_shape,
        scratch_shapes=[pltpu.VMEM((tl + SUBLANES, B_WIDTH), F32), pltpu.VMEM((1, B_WIDTH), F32)],
        compiler_params=_cparams(("parallel", "arbitrary")), name="ab_mixer")
    return call(x_all, conv0, h0, *weights)


def _out_ln_kernel(x_ref, o_ref, w_ref, g_ref, b_ref, y_ref):
    mix = _dot(o_ref[...].astype(BF16), w_ref[...])
    y_ref[...] = _layer_norm(DN_ALPHA * x_ref[...] + mix, g_ref[...], b_ref[...])


def _out_ln(x_all, row_off, o, w_bf16, g, b, tm=ROW_TM):
    T = x_all.shape[0]
    n, K = o.shape
    tm = min(tm, n)
    off = row_off // tm
    rows = pl.BlockSpec((tm, D_MODEL), lambda i: (off + i, 0))
    in_specs = [rows, pl.BlockSpec((tm, K), lambda i: (i, 0)), _full(w_bf16.shape), _full(g.shape), _full(b.shape)]
    call = _call_inplace(
        _out_ln_kernel,
        grid=(n // tm,), in_specs=in_specs, out_specs=rows,
        out_shape=jax.ShapeDtypeStruct((T, D_MODEL), F32),
        compiler_params=_cparams(("parallel",)), name="out_ln")
    return call(x_all, o, w_bf16, g, b)


def _fox_pre_p_kernel(x_ref, wkt_ref, wvt_ref, wqa_ref, wka_ref, wf_ref, bf_ref, pq_ref, pk_ref,
                      kt_ref, vt_ref, lft_ref, qa_ref, ka_ref, carry_s, *, tm):
    @pl.when(pl.program_id(1) == 0)
    def _():
        carry_s[...] = jnp.zeros(carry_s.shape, F32)

    x = x_ref[...]
    xb = x.astype(BF16)
    kt_ref[...] = _dot_nt(wkt_ref[...], xb)
    vt_ref[...] = _dot_nt(wvt_ref[...], xb)

    f = _dot_x3(x, wf_ref[...]) + bf_ref[...]
    lane = lax.broadcasted_iota(jnp.int32, (tm, LANES), 1)
    logf = jnp.where(lane < C_HEADS, _log_sigmoid(f), 0.0)
    lft_ref[...] = logf.T[0:C_HEADS, :]

    c = logf
    rows = lax.broadcasted_iota(jnp.int32, (tm, LANES), 0)
    k = 1
    while k < tm:
        c = c + jnp.where(rows >= k, pltpu.roll(c, k, 0), 0.0)
        k *= 2
    c = c + carry_s[...]
    carry_s[...] = c[tm - 1:tm, :]

    hi = c.astype(BF16).astype(F32)
    r1 = c - hi
    mid = r1.astype(BF16).astype(F32)
    lo = (r1 - mid).astype(BF16).astype(F32)
    pieces = (hi + pltpu.roll(mid, C_HEADS, 1) + pltpu.roll(lo, 2 * C_HEADS, 1)
              + jnp.where(lane == ONE_LANE, 1.0, 0.0)).astype(BF16)
    qa_ref[...] = (_dot(xb, wqa_ref[...]) + _dot(pieces, pq_ref[...])).astype(BF16)
    ka_ref[...] = (_dot(xb, wka_ref[...]) + _dot(pieces, pk_ref[...])).astype(BF16)


def _piece_placement():
    pq = np.zeros((LANES, QA_W), np.float32)
    pk = np.zeros((LANES, QA_W), np.float32)
    for h in range(C_HEADS):
        base = h * SLOT + C_HEAD_DIM
        for p in range(N_PIECE):
            pq[p * C_HEADS + h, base + p] = 1.0
            pq[ONE_LANE, base + N_PIECE + p] = 1.0
            pk[ONE_LANE, base + p] = 1.0
            pk[p * C_HEADS + h, base + N_PIECE + p] = -1.0
    return jnp.asarray(pq, BF16), jnp.asarray(pk, BF16)


def _slot_weights(w, scale):
    w = (w * scale).reshape(D_MODEL, C_HEADS, C_HEAD_DIM)
    w = jnp.pad(w, ((0, 0), (0, 0), (0, SLOT - C_HEAD_DIM)))
    return w.reshape(D_MODEL, QA_W).astype(BF16)


def _fox_pre_p(x_all, B, L, w, j, kt_prev, vt_prev, lft_prev, tm=FOX_TM):
    nl = L // tm
    weights = [w["wkt"], w["wvt"], w["wqa"], w["wka"], w["w_f"], w["b_f"], w["pq"], w["pk"]]
    in_specs = [pl.BlockSpec((tm, D_MODEL), lambda b, l: (b * nl + l, 0))] + [_full(a.shape) for a in weights]
    t_spec = lambda n: pl.BlockSpec((None, None, n, tm), lambda b, l: (j, b, 0, l))
    a_spec = pl.BlockSpec((tm, QA_W), lambda b, l: (b * nl + l, 0))
    out_specs = [t_spec(D_MODEL), t_spec(D_MODEL), t_spec(C_HEADS), a_spec, a_spec]
    out_shape = [jax.ShapeDtypeStruct((N_ODD, B, D_MODEL, L), F32), jax.ShapeDtypeStruct((N_ODD, B, D_MODEL, L), F32),
                 jax.ShapeDtypeStruct((N_ODD, B, C_HEADS, L), F32),
                 jax.ShapeDtypeStruct((B * L, QA_W), BF16), jax.ShapeDtypeStruct((B * L, QA_W), BF16)]
    kern = functools.partial(_fox_pre_p_kernel, tm=tm)
    kw = dict(grid=(B, nl), out_specs=out_specs, out_shape=out_shape,
              scratch_shapes=[pltpu.VMEM((1, LANES), F32)],
              compiler_params=_cparams(("parallel", "arbitrary")), name="fox_pre_p")
    n_in = len(in_specs)

    def with_prev(*refs):
        return kern(*refs[:n_in], *refs[n_in + 3:])
    return pl.pallas_call(with_prev, in_specs=in_specs + [_any()] * 3,
                          input_output_aliases={n_in: 0, n_in + 1: 1, n_in + 2: 2}, **kw)(
        x_all, *weights, kt_prev, vt_prev, lft_prev)


def _fox_attn_p_kernel(qa_ref, ka_ref, vt_ref, o_ref, acc_s, *, t, tk):
    qi = pl.program_id(2)
    per = t // tk
    qas = [qa_ref[:, hh * SLOT:(hh + 1) * SLOT] for hh in range(2)]
    acc_s[...] = jnp.zeros(acc_s.shape, F32)
    krow = lax.broadcasted_iota(jnp.int32, (tk, t), 0)
    qcol = lax.broadcasted_iota(jnp.int32, (tk, t), 1)

    def scores(off):
        return tuple(_dot_nt(ka_ref[pl.ds(off, tk), hh * SLOT:(hh + 1) * SLOT], qas[hh]) for hh in range(2))

    def consume(off, ss, carry, diag):
        new = []
        for hh in range(2):
            m_old, l_old = carry[2 * hh], carry[2 * hh + 1]
            s = ss[hh]
            if diag is not None:
                s = jnp.where(krow + diag * tk <= qcol, s, NEG_INF)
            m_new = jnp.maximum(m_old, jnp.max(s, axis=0, keepdims=True))
            p = jnp.exp(s - m_new)
            alpha = jnp.exp(m_old - m_new)
            l_new = alpha * l_old + jnp.sum(p, axis=0, keepdims=True)
            vc = vt_ref[hh * C_HEAD_DIM:(hh + 1) * C_HEAD_DIM, pl.ds(off, tk)].astype(BF16)
            acc_s[hh] = alpha * acc_s[hh] + _dot(vc, p.astype(BF16))
            new += [m_new, l_new]
        return tuple(new)

    def body(jj, c):
        nxt = scores(pl.multiple_of((jj + 1) * tk, tk))
        return consume(pl.multiple_of(jj * tk, tk), c[4:], c[:4], None) + nxt

    init = (jnp.full((1, t), NEG_INF, F32), jnp.zeros((1, t), F32)) * 2
    c = lax.fori_loop(0, qi * per, body, init + scores(0))
    for d in range(per):
        off = pl.multiple_of(qi * t + d * tk, tk)
        nxt = scores(pl.multiple_of(qi * t + (d + 1) * tk, tk)) if d + 1 < per else ()
        c = consume(off, c[4:], c[:4], d) + nxt
    o_t = jnp.concatenate([acc_s[0] / c[1], acc_s[1] / c[3]], axis=0)
    o_ref[...] = o_t.T


def _fox_attn_p(qa, ka, vt, j, B, L, t=ATT_T, tk=ATT_TK):
    nq = L // t
    n_hp = C_HEADS // 2
    return pl.pallas_call(
        functools.partial(_fox_attn_p_kernel, t=t, tk=tk),
        grid=(B, n_hp, nq),
        in_specs=[pl.BlockSpec((t, 2 * SLOT), lambda b, h, i: (b * nq + i, h)),
                  pl.BlockSpec((L, 2 * SLOT), lambda b, h, i: (b, h)),
                  pl.BlockSpec((None, None, 2 * C_HEAD_DIM, L), lambda b, h, i: (j, b, h, 0))],
        out_specs=pl.BlockSpec((t, LANES), lambda b, h, i: (b * nq + i, h)),
        out_shape=jax.ShapeDtypeStruct((B * L, D_MODEL), F32),
        scratch_shapes=[pltpu.VMEM((2, C_HEAD_DIM, t), F32)],
        compiler_params=_cparams(("parallel", "parallel", "arbitrary")),
        name="fox_attn_p",
    )(qa, ka, vt)


def _fox_pre_s_kernel(x_ref, w_ref, wf_ref, bf_ref, q_ref, k_ref, v_ref, lf_ref):
    x = x_ref[...]
    xb = x.astype(BF16)
    q_ref[...] = _dot(xb, w_ref[:, 0:D_MODEL])
    k_ref[...] = _dot(xb, w_ref[:, D_MODEL:2 * D_MODEL])
    v_ref[...] = _dot(xb, w_ref[:, 2 * D_MODEL:3 * D_MODEL])
    f = _dot_x3(x, wf_ref[...]) + bf_ref[...]
    lf_ref[...] = _log_sigmoid(f)[:, 0:C_HEADS]


def _fox_pre_s(x_all, row_off, n, w_qkv, w_f, b_f, tm=ROW_TM):
    tm = min(tm, n)
    off = row_off // tm
    row = lambda c: pl.BlockSpec((tm, c), lambda i: (i, 0))
    return pl.pallas_call(
        _fox_pre_s_kernel,
        grid=(n // tm,),
        in_specs=[pl.BlockSpec((tm, D_MODEL), lambda i: (off + i, 0)),
                  _full(w_qkv.shape), _full(w_f.shape), _full(b_f.shape)],
        out_specs=[row(D_MODEL), row(D_MODEL), row(D_MODEL), row(C_HEADS)],
        out_shape=[jax.ShapeDtypeStruct((n, D_MODEL), F32)] * 3 + [jax.ShapeDtypeStruct((n, C_HEADS), F32)],
        compiler_params=_cparams(("parallel",)),
        name="fox_pre_s",
    )(x_all, w_qkv, w_f, b_f)


def _cumsum_kernel(x_ref, o_ref):
    x = x_ref[...]
    n = x.shape[-1]
    lane = lax.broadcasted_iota(jnp.int32, x.shape, 1)
    k = 1
    while k < n:
        x = x + jnp.where(lane >= k, pltpu.roll(x, k, 1), 0.0)
        k *= 2
    o_ref[...] = x


def _cumsum_lanes(x):
    B, H, Lp = x.shape
    spec = pl.BlockSpec((None, H, Lp), lambda b: (b, 0, 0))
    return pl.pallas_call(
        _cumsum_kernel, grid=(B,), in_specs=[spec], out_specs=spec,
        out_shape=jax.ShapeDtypeStruct(x.shape, F32),
        compiler_params=_cparams(("parallel",)), name="logf_cumsum",
    )(x)


def _fox_attn_s_kernel(q_ref, kn_ref, vn_ref, kt_ref, vt_ref, cq_ref, ck_ref, o_ref, *, L, P):
    row = lax.broadcasted_iota(jnp.int32, (2 * L, LANES), 0)
    lane = lax.broadcasted_iota(jnp.int32, (2 * L, LANES), 1)
    q = q_ref[...] * C_SCALE
    q2 = jnp.concatenate([q, q], axis=0)
    qb = jnp.where((row < L) == (lane < C_HEAD_DIM), q2, 0.0).astype(BF16)
    s_p = _dot(qb, kt_ref[...].astype(BF16))
    s_n = _dot_nt(qb, kn_ref[...].astype(BF16))
    cq2 = jnp.concatenate([cq_ref[0], cq_ref[1]], axis=0)
    first_p = lax.broadcasted_iota(jnp.int32, (2 * L, P), 0) < L
    s_p = s_p + (cq2 - jnp.where(first_p, ck_ref[0:1, 0:P], ck_ref[1:2, 0:P]))
    rn = lax.broadcasted_iota(jnp.int32, (2 * L, L), 0)
    cn = lax.broadcasted_iota(jnp.int32, (2 * L, L), 1)
    s_n = s_n + (cq2 - jnp.where(rn < L, ck_ref[0:1, P:P + L], ck_ref[1:2, P:P + L]))
    s_n = jnp.where(cn <= jnp.where(rn < L, rn, rn - L), s_n, NEG_INF)
    m = jnp.maximum(jnp.max(s_p, axis=-1, keepdims=True), jnp.max(s_n, axis=-1, keepdims=True))
    p_p = jnp.exp(s_p - m)
    p_n = jnp.exp(s_n - m)
    den = jnp.sum(p_p, axis=-1, keepdims=True) + jnp.sum(p_n, axis=-1, keepdims=True)
    o2 = (_dot_nt(p_p.astype(BF16), vt_ref[...].astype(BF16))
          + _dot(p_n.astype(BF16), vn_ref[...].astype(BF16))) / den
    first_lanes = lax.broadcasted_iota(jnp.int32, (L, LANES), 1) < C_HEAD_DIM
    o_ref[...] = jnp.where(first_lanes, o2[0:L], o2[L:2 * L])


def _fox_attn_s(q, k, v, kt_cache, vt_cache, j, cq, ck, B, L):
    P = kt_cache.shape[-1]
    n_hp = C_HEADS // 2
    row = pl.BlockSpec((L, LANES), lambda b, h: (b, h))
    cache = pl.BlockSpec((None, None, 2 * C_HEAD_DIM, P), lambda b, h: (j, b, h, 0))
    return pl.pallas_call(
        functools.partial(_fox_attn_s_kernel, L=L, P=P),
        grid=(B, n_hp),
        in_specs=[row, row, row, cache, cache,
                  pl.BlockSpec((None, None, 2, L, 1), lambda b, h: (b, h, 0, 0, 0)),
                  pl.BlockSpec((None, None, 2, ck.shape[-1]), lambda b, h: (b, h, 0, 0))],
        out_specs=row,
        out_shape=jax.ShapeDtypeStruct((B * L, D_MODEL), F32),
        compiler_params=_cparams(("parallel", "parallel")),
        name="fox_attn_s",
    )(q, k, v, kt_cache, vt_cache, cq, ck)


def _mem_kernel(x_ref, mk_ref, mv_ref, wq_ref, wo_ref, g_ref, b_ref, y_ref):
    x = x_ref[...]
    q = _dot(x.astype(BF16), wq_ref[...]).astype(BF16)
    mv_t = mv_ref[...].T.astype(BF16)
    o_t = []
    for h in range(M_HEADS):
        lo, hi = h * M_HEAD_DIM, (h + 1) * M_HEAD_DIM
        s = _dot_nt(mk_ref[:, lo:hi].astype(BF16), q[:, lo:hi]) * M_SCALE
        p = jnp.exp(s - jnp.max(s, axis=0, keepdims=True))
        den = jnp.sum(p, axis=0, keepdims=True)
        o_t.append(_dot(mv_t[lo:hi, :], p.astype(BF16)) / den)
    o = jnp.concatenate(o_t, axis=0).T
    out = DN_ALPHA * x + _dot(o.astype(BF16), wo_ref[...])
    y_ref[...] = _layer_norm(out, g_ref[...], b_ref[...])


def _mem_layer(x_all, row_off, B, L, mk, mv, layer, w_q, w_o, g, b, tl=256):
    T = x_all.shape[0]
    tl = min(tl, L)
    nl = L // tl
    off = row_off // tl
    rows = pl.BlockSpec((tl, D_MODEL), lambda bb, l: (off + bb * nl + l, 0))
    mem = pl.BlockSpec((None, None, N_MEM, M_WIDTH), lambda bb, l: (layer, bb, 0, 0))
    in_specs = [rows, mem, mem, _full(w_q.shape), _full(w_o.shape), _full(g.shape), _full(b.shape)]
    call = _call_inplace(
        _mem_kernel,
        grid=(B, nl), in_specs=in_specs, out_specs=rows,
        out_shape=jax.ShapeDtypeStruct((T, D_MODEL), F32),
        compiler_params=_cparams(("parallel", "parallel")), name="mem_attend")
    return call(x_all, mk, mv, w_q, w_o, g, b)


def _router_kernel(x_ref, wr_ref, br_ref, mi_ref, mw_ref, cnt_ref, carry_s, *, tm):
    @pl.when(pl.program_id(0) == 0)
    def _():
        carry_s[...] = jnp.zeros(carry_s.shape, F32)

    x = x_ref[...]
    logits = _dot_x3(x, wr_ref[...]) + br_ref[...]
    lane = lax.broadcasted_iota(jnp.int32, (tm, LANES), 1)
    work = jnp.where(lane < N_EXPERTS, logits, NEG_INF)
    vals, idxs = [], []
    for _ in range(TOP_K):
        mx = jnp.max(work, axis=-1, keepdims=True)
        idx = jnp.min(jnp.where(work == mx, lane, LANES), axis=-1, keepdims=True)
        vals.append(mx)
        idxs.append(idx)
        work = jnp.where(lane == idx, NEG_INF, work)
    es = [jnp.exp(vv - vals[0]) for vv in vals]
    den = es[0] + es[1] + es[2] + es[3]
    sel = jnp.zeros((tm, LANES), F32)
    for idx in idxs:
        sel = jnp.where(lane == idx, 1.0, sel)
    rr = lax.broadcasted_iota(jnp.int32, (tm, tm), 0)
    cc = lax.broadcasted_iota(jnp.int32, (tm, tm), 1)
    strict = jnp.where(cc < rr, 1.0, 0.0).astype(BF16)
    rank = carry_s[...] + _dot(strict, sel.astype(BF16))
    mi = jnp.zeros((tm, LANES), jnp.int32)
    mw = jnp.zeros((tm, LANES), F32)
    for kk in range(TOP_K):
        rk = jnp.sum(jnp.where(lane == idxs[kk], rank, 0.0), axis=-1, keepdims=True)
        mi = jnp.where(lane == kk, idxs[kk], mi)
        mi = jnp.where(lane == TOP_K + kk, rk.astype(jnp.int32), mi)
        mw = jnp.where(lane == kk, es[kk] / den, mw)
    mi_ref[...] = mi
    mw_ref[...] = mw
    total = carry_s[...] + jnp.sum(sel, axis=0, keepdims=True)
    carry_s[...] = total
    cnt_ref[...] = total


def _router(x, w_r, b_r, tm=ROW_TM):
    T = x.shape[0]
    tm = min(tm, T)
    row = lambda: pl.BlockSpec((tm, LANES), lambda i: (i, 0))
    return pl.pallas_call(
        functools.partial(_router_kernel, tm=tm),
        grid=(T // tm,),
        in_specs=[pl.BlockSpec((tm, D_MODEL), lambda i: (i, 0)), _full(w_r.shape), _full(b_r.shape)],
        out_specs=[row(), row(), _full((1, LANES))],
        out_shape=[jax.ShapeDtypeStruct((T, LANES), jnp.int32), jax.ShapeDtypeStruct((T, LANES), F32),
                   jax.ShapeDtypeStruct((1, LANES), F32)],
        scratch_shapes=[pltpu.VMEM((1, LANES), F32)],
        compiler_params=_cparams(("arbitrary",)),
        name="router",
    )(x, w_r, b_r)


def _row_copy(src, src_row, dst, dst_row, sem):
    return pltpu.make_async_copy(src.at[pl.ds(src_row, 1)], dst.at[pl.ds(dst_row, 1)], sem)


def _dispatch_kernel(pos_ref, x_ref, xs_in_ref, xs_ref, sem, *, tm):
    del xs_in_ref
    base = pl.program_id(0) * tm * TOP_K

    def issue(t, c):
        for kk in range(TOP_K):
            _row_copy(x_ref, t, xs_ref, pos_ref[base + t * TOP_K + kk], sem).start()
        return c
    lax.fori_loop(0, tm, issue, 0)

    def drain(t, c):
        for kk in range(TOP_K):
            _row_copy(x_ref, 0, xs_ref, 0, sem).wait()
        return c
    lax.fori_loop(0, tm, drain, 0)


def _dispatch(pos_flat, x, xs_buf, tm=DISP_TM):
    T = x.shape[0]
    tm = min(tm, T)
    return pl.pallas_call(
        functools.partial(_dispatch_kernel, tm=tm),
        grid_spec=pltpu.PrefetchScalarGridSpec(
            num_scalar_prefetch=1,
            grid=(T // tm,),
            in_specs=[pl.BlockSpec((tm, D_MODEL), lambda i, pos: (i, 0)), _any()],
            out_specs=_any(),
            scratch_shapes=[pltpu.SemaphoreType.DMA(())],
        ),
        out_shape=jax.ShapeDtypeStruct(xs_buf.shape, F32),
        input_output_aliases={2: 0},
        compiler_params=_cparams(("arbitrary",)),
        name="moe_dispatch",
    )(pos_flat, x, xs_buf)


def _gmm_kernel(te_ref, nu_ref, xs_ref, wgu_ref, bgu_ref, wdn_ref, bdn_ref, ys_ref, wgu_s, wdn_s):
    i = pl.program_id(0)

    @pl.when((i == 0) | (te_ref[i] != te_ref[jnp.maximum(i - 1, 0)]))
    def _():
        def cast_rows(c, carry):
            r = pl.multiple_of(c * LANES, LANES)
            wgu_s[pl.ds(r, LANES), :] = wgu_ref[pl.ds(r, LANES), :].astype(BF16)
            wdn_s[pl.ds(r, LANES), :] = wdn_ref[pl.ds(r, LANES), :].astype(BF16)
            return carry
        lax.fori_loop(0, D_MODEL // LANES, cast_rows, 0)

    @pl.when(i < nu_ref[0])
    def _():
        xb = xs_ref[...].astype(BF16)
        gu = _dot(xb, wgu_s[...]) + bgu_ref[...]
        g = jnp.minimum(gu[:, 0:D_EXPERT], SWIGLU_LIMIT)
        u = jnp.clip(gu[:, D_EXPERT:], -SWIGLU_LIMIT, SWIGLU_LIMIT)
        h = (u + 1.0) * g * _sigmoid(SWIGLU_ALPHA * g)
        ys_ref[...] = _dot(h.astype(BF16), wdn_s[...]) + bdn_ref[...]

    @pl.when(i >= nu_ref[0])
    def _():
        ys_ref[...] = jnp.zeros(ys_ref.shape, F32)


def _gmm(tile_expert, n_used, xs, w_gu, b_gu, w_dn, b_dn, layer, tm=MOE_TM):
    R = xs.shape[0]
    wmap = lambda i, te, nu: (layer, te[i], 0, 0)
    return pl.pallas_call(
        _gmm_kernel,
        grid_spec=pltpu.PrefetchScalarGridSpec(
            num_scalar_prefetch=2,
            grid=(R // tm,),
            in_specs=[pl.BlockSpec((tm, D_MODEL), lambda i, te, nu: (i, 0)),
                      pl.BlockSpec((None, None, D_MODEL, 2 * D_EXPERT), wmap),
                      pl.BlockSpec((None, None, 1, 2 * D_EXPERT), wmap),
                      pl.BlockSpec((None, None, D_EXPERT, D_MODEL), wmap),
                      pl.BlockSpec((None, None, 1, D_MODEL), wmap)],
            out_specs=pl.BlockSpec((tm, D_MODEL), lambda i, te, nu: (i, 0)),
            scratch_shapes=[pltpu.VMEM((D_MODEL, 2 * D_EXPERT), BF16), pltpu.VMEM((D_EXPERT, D_MODEL), BF16)],
        ),
        out_shape=jax.ShapeDtypeStruct((R, D_MODEL), F32),
        compiler_params=_cparams(("arbitrary",)),
        name="moe_gmm",
    )(tile_expert, n_used, xs, w_gu, b_gu, w_dn, b_dn)


def _combine_kernel(pos_ref, x_ref, mw_ref, ys_ref, g_ref, b_ref, y_ref, buf, sem, *, tm):
    base = pl.program_id(0) * tm * TOP_K

    def issue(t, c):
        for kk in range(TOP_K):
            _row_copy(ys_ref, pos_ref[base + t * TOP_K + kk], buf.at[kk], t, sem).start()
        return c
    lax.fori_loop(0, tm, issue, 0)

    def drain(t, c):
        for kk in range(TOP_K):
            _row_copy(ys_ref, 0, buf.at[kk], 0, sem).wait()
        return c
    lax.fori_loop(0, tm, drain, 0)

    out = DN_ALPHA * x_ref[...]
    for kk in range(TOP_K):
        out = out + mw_ref[:, kk:kk + 1] * buf[kk]
    y_ref[...] = _layer_norm(out, g_ref[...], b_ref[...])


def _combine(pos_flat, x, meta_w, ys, g, b, tm=DISP_TM):
    T = x.shape[0]
    tm = min(tm, T)
    return pl.pallas_call(
        functools.partial(_combine_kernel, tm=tm),
        grid_spec=pltpu.PrefetchScalarGridSpec(
            num_scalar_prefetch=1,
            grid=(T // tm,),
            in_specs=[pl.BlockSpec((tm, D_MODEL), lambda i, pos: (i, 0)),
                      pl.BlockSpec((tm, LANES), lambda i, pos: (i, 0)),
                      _any(),
                      pl.BlockSpec((1, D_MODEL), lambda i, pos: (0, 0)),
                      pl.BlockSpec((1, D_MODEL), lambda i, pos: (0, 0))],
            out_specs=pl.BlockSpec((tm, D_MODEL), lambda i, pos: (i, 0)),
            scratch_shapes=[pltpu.VMEM((TOP_K, tm, D_MODEL), F32), pltpu.SemaphoreType.DMA(())],
        ),
        out_shape=jax.ShapeDtypeStruct((T, D_MODEL), F32),
        compiler_params=_cparams(("arbitrary",)),
        name="moe_combine",
    )(pos_flat, x, meta_w, ys, g, b)


def _moe_layer(x, xs_buf, w, layer, tm=MOE_TM):
    T = x.shape[0]
    R = xs_buf.shape[0]
    n_tiles = R // tm
    meta_i, meta_w, counts = _router(x, w["w_r"], w["b_r"])
    cnt = counts[0, :N_EXPERTS].astype(jnp.int32)
    tiles_per = (cnt + tm - 1) // tm
    tile_end = jnp.cumsum(tiles_per)
    group_start = (tile_end - tiles_per) * tm
    n_used = tile_end[-1:]
    tile_ids = jnp.arange(n_tiles, dtype=jnp.int32)
    tile_expert = jnp.sum(tile_ids[:, None] >= tile_end[None, :], axis=1).astype(jnp.int32)
    last_expert = jnp.sum(jnp.maximum(n_used - 1, 0) >= tile_end).astype(jnp.int32)
    tile_expert = jnp.where(tile_ids < n_used, tile_expert, last_expert)
    pos = group_start[meta_i[:, :TOP_K]] + meta_i[:, TOP_K:2 * TOP_K]
    pos_flat = pos.reshape(T * TOP_K).astype(jnp.int32)
    xs = _dispatch(pos_flat, x, xs_buf)
    ys = _gmm(tile_expert, n_used.astype(jnp.int32), xs, w["w_gu"], w["b_gu"], w["w_dn"], w["b_dn"], layer, tm)
    return _combine(pos_flat, x, meta_w, ys, w["g3"], w["b3"]), xs


def _block_diag(w):
    eye = jnp.eye(B_BLOCKS, dtype=w.dtype)
    return jnp.einsum("hij,hg->higj", w, eye).reshape(B_WIDTH, B_WIDTH)


def _pad_lanes(a, value=0.0):
    return jnp.pad(a, ((0, 0),) * (a.ndim - 1) + ((0, LANES - a.shape[-1]),), constant_values=value)


def kernel(x_prompt, x_sample, state_b_conv, state_b_h, cache_c_k, cache_c_v, cache_c_logf, cache_mem_k, cache_mem_v, mem_prompt, w_in_ab, a_ln_g, a_ln_b, a_w_s, a_b_s, b_conv_w, b_conv_b, b_w_a, b_b_a, b_w_x, b_b_x, b_lambda, w_out_ab, c_w_in, c_b_f, c_w_out, w_mq, w_mk, w_mv, w_mo, w_router, b_router, w_gu, b_gu, w_dn, b_dn, ln1_g, ln1_b, ln2_g, ln2_b, ln3_g, ln3_b):
    Bp, Lp, _ = x_prompt.shape
    Bs, Ls, _ = x_sample.shape
    P = cache_c_k.shape[2]
    Tp, Ts = Bp * Lp, Bs * Ls
    T = Tp + Ts
    row = lambda a: a.reshape(1, -1)

    mem_flat = mem_prompt.reshape(Bp * N_MEM, D_MODEL)
    w_mkv = jnp.concatenate([w_mk, w_mv], axis=-1).astype(BF16)
    p_mem = [_proj(mem_flat, w_mkv[l], n_out=2) for l in range(DEPTH)]
    p_mem_k = jnp.stack([p[0] for p in p_mem]).reshape(DEPTH, Bp, N_MEM, M_WIDTH)
    p_mem_v = jnp.stack([p[1] for p in p_mem]).reshape(DEPTH, Bp, N_MEM, M_WIDTH)
    s_mem_k = cache_mem_k.reshape(DEPTH, Bs, N_MEM, M_WIDTH)
    s_mem_v = cache_mem_v.reshape(DEPTH, Bs, N_MEM, M_WIDTH)

    kt_cache = jnp.transpose(cache_c_k, (0, 1, 3, 4, 2)).reshape(N_ODD, Bs, D_MODEL, P)
    vt_cache = jnp.transpose(cache_c_v, (0, 1, 3, 4, 2)).reshape(N_ODD, Bs, D_MODEL, P)
    lft_cache = jnp.transpose(cache_c_logf, (0, 1, 3, 2))
    pq, pk = _piece_placement()

    b_gu4 = b_gu[:, :, None, :]
    b_dn4 = b_dn[:, :, None, :]
    R = T * TOP_K + N_EXPERTS * MOE_TM
    xs_buf = jnp.zeros((R, D_MODEL), F32)

    x_all = jnp.concatenate([x_prompt.reshape(Tp, D_MODEL), x_sample.reshape(Ts, D_MODEL)], axis=0)
    kt_p = jnp.zeros((N_ODD, Bp, D_MODEL, Lp), F32)
    vt_p = jnp.zeros((N_ODD, Bp, D_MODEL, Lp), F32)
    lft_p = jnp.zeros((N_ODD, Bp, C_HEADS, Lp), F32)
    st_p = {"b_conv": [], "b_h": []}
    st_s = {"a_v": [], "b_conv": [], "b_h": [], "c_k": [], "c_v": [], "c_logf": []}
    for layer in range(DEPTH):
        j = layer // 2
        g1, b1 = row(ln1_g[layer]), row(ln1_b[layer])
        if layer % 2 == 0:
            w = dict(w_in=w_in_ab[j].astype(BF16), ln_g=row(a_ln_g[j]), ln_b=row(a_ln_b[j]), w_s=a_w_s[j],
                     b_st=a_b_s[j].T, conv_w=b_conv_w[j], conv_b=row(b_conv_b[j]),
                     wa=_block_diag(b_w_a[j]).astype(BF16), ba=row(b_b_a[j]),
                     wx=_block_diag(b_w_x[j]).astype(BF16), bx=row(b_b_x[j]), lam=row(b_lambda[j]),
                     w_out=w_out_ab[j].astype(BF16), g1=g1, b1=b1)
            conv0_p = jnp.zeros((Bp, SUBLANES, B_WIDTH), F32)
            h0_p = jnp.zeros((Bp, 1, B_WIDTH), F32)
            x_all, buf_p, hl_p = _ab_layer(x_all, 0, Bp, Lp, conv0_p, h0_p, w, False)
            conv0_s = jnp.pad(state_b_conv[j], ((0, 0), (SUBLANES - (B_CONV - 1), 0), (0, 0)))
            x_all, v_s, buf_s, hl_s = _ab_layer(x_all, Tp, Bs, Ls, conv0_s, state_b_h[j][:, None, :], w, True)
            st_p["b_conv"].append(buf_p[:, SUBLANES - (B_CONV - 1):])
            st_p["b_h"].append(hl_p[:, 0])
            st_s["a_v"].append(v_s.reshape(Bs, Ls, A_WIDTH))
            st_s["b_conv"].append(buf_s[:, SUBLANES - (B_CONV - 1):])
            st_s["b_h"].append(hl_s[:, 0])
        else:
            wq, wk, wv = (c_w_in[j][:, i * D_MODEL:(i + 1) * D_MODEL] for i in range(3))
            w_f = _pad_lanes(c_w_in[j][:, 3 * D_MODEL:])
            b_f = _pad_lanes(row(c_b_f[j]))
            w_out = c_w_out[j].astype(BF16)
            wp = dict(wkt=wk.T.astype(BF16), wvt=wv.T.astype(BF16), wqa=_slot_weights(wq, C_SCALE),
                      wka=_slot_weights(wk, 1.0), w_f=w_f, b_f=b_f, pq=pq, pk=pk)
            kt_p, vt_p, lft_p, qa, ka = _fox_pre_p(x_all, Bp, Lp, wp, j, kt_p, vt_p, lft_p)
            o_p = _fox_attn_p(qa, ka, vt_p, j, Bp, Lp)
            q_s, k_s, v_s2, lf_s = _fox_pre_s(x_all, Tp, Ts, c_w_in[j][:, :3 * D_MODEL].astype(BF16), w_f, b_f)
            tot = P + Ls
            tot_pad = -(-tot // LANES) * LANES
            lf_t = jnp.concatenate([lft_cache[j], lf_s.reshape(Bs, Ls, C_HEADS).transpose(0, 2, 1),
                                    jnp.zeros((Bs, C_HEADS, tot_pad - tot), F32)], axis=2)
            cum = _cumsum_lanes(lf_t)
            ck = cum.reshape(Bs, C_HEADS // 2, 2, tot_pad)
            cq = cum[:, :, P:P + Ls].reshape(Bs, C_HEADS // 2, 2, Ls, 1)
            o_s = _fox_attn_s(q_s, k_s, v_s2, kt_cache, vt_cache, j, cq, ck, Bs, Ls)
            x_all = _out_ln(x_all, 0, o_p, w_out, g1, b1)
            x_all = _out_ln(x_all, Tp, o_s, w_out, g1, b1)
            st_s["c_k"].append(k_s.reshape(Bs, Ls, C_HEADS, C_HEAD_DIM))
            st_s["c_v"].append(v_s2.reshape(Bs, Ls, C_HEADS, C_HEAD_DIM))
            st_s["c_logf"].append(lf_s.reshape(Bs, Ls, C_HEADS))
        g2, b2 = row(ln2_g[layer]), row(ln2_b[layer])
        wq_m, wo_m = w_mq[layer].astype(BF16), w_mo[layer].astype(BF16)
        x_all = _mem_layer(x_all, 0, Bp, Lp, p_mem_k, p_mem_v, layer, wq_m, wo_m, g2, b2)
        x_all = _mem_layer(x_all, Tp, Bs, Ls, s_mem_k, s_mem_v, layer, wq_m, wo_m, g2, b2)
        wm = dict(w_r=_pad_lanes(w_router[layer]), b_r=_pad_lanes(row(b_router[layer])),
                  w_gu=w_gu, b_gu=b_gu4, w_dn=w_dn, b_dn=b_dn4,
                  g3=row(ln3_g[layer]), b3=row(ln3_b[layer]))
        x_all, xs_buf = _moe_layer(x_all, xs_buf, wm, layer)

    p_c_k = jnp.transpose(kt_p.reshape(N_ODD, Bp, C_HEADS, C_HEAD_DIM, Lp), (0, 1, 4, 2, 3))
    p_c_v = jnp.transpose(vt_p.reshape(N_ODD, Bp, C_HEADS, C_HEAD_DIM, Lp), (0, 1, 4, 2, 3))
    p_c_logf = jnp.transpose(lft_p, (0, 1, 3, 2))
    return (x_all[:Tp].reshape(Bp, Lp, D_MODEL), x_all[Tp:].reshape(Bs, Ls, D_MODEL),
            jnp.stack(st_p["b_conv"]), jnp.stack(st_p["b_h"]),
            p_c_k, p_c_v, p_c_logf,
            p_mem_k.reshape(DEPTH, Bp, N_MEM, M_HEADS, M_HEAD_DIM),
            p_mem_v.reshape(DEPTH, Bp, N_MEM, M_HEADS, M_HEAD_DIM),
            jnp.stack(st_s["a_v"]), jnp.stack(st_s["b_conv"]), jnp.stack(st_s["b_h"]),
            jnp.stack(st_s["c_k"]), jnp.stack(st_s["c_v"]), jnp.stack(st_s["c_logf"]))
```

```python
import functools

import numpy as np
import jax
import jax.numpy as jnp
from jax import lax
from jax.experimental import pallas as pl
from jax.experimental.pallas import tpu as pltpu

F32 = jnp.float32
BF16 = jnp.bfloat16

D_MODEL = 1024
DEPTH = 4
N_ODD = DEPTH // 2
A_WIDTH = 512
A_GROUPS = 4
A_GROUP_DIM = 128
A_CHUNK = 128
B_WIDTH = 512
B_BLOCKS = 8
B_BLOCK_DIM = 64
B_CONV = 4
B_C = 8.0
C_HEADS = 16
C_HEAD_DIM = 64
C_SCALE = C_HEAD_DIM ** -0.5
N_MEM = 256
M_HEADS = 4
M_HEAD_DIM = 128
M_WIDTH = 512
M_SCALE = M_HEAD_DIM ** -0.5
N_EXPERTS = 32
TOP_K = 4
D_EXPERT = 1024
SWIGLU_LIMIT = 7.0
SWIGLU_ALPHA = 1.702
DN_ALPHA = (2 * DEPTH) ** 0.25
LN_EPS = 1e-5

LANES = 128
SUBLANES = 8
VMEM_LIMIT = 56 * 1024 * 1024
NEG_INF = float("-inf")

MOE_TM = 256
ROW_TM = 512
FOX_TM = 256
ATT_T = 512
ATT_TK = 256
DISP_TM = 256

SLOT = LANES
QA_W = C_HEADS * SLOT
N_PIECE = 3
ONE_LANE = N_PIECE * C_HEADS


def _cparams(sem):
    return pltpu.CompilerParams(dimension_semantics=sem, vmem_limit_bytes=VMEM_LIMIT)


def _dot(a, b):
    return jnp.dot(a, b, preferred_element_type=F32)


def _dot_nt(a, b):
    return lax.dot_general(a, b, (((1,), (1,)), ((), ())), preferred_element_type=F32)


def _dot_x3(x, w):
    x_hi = x.astype(BF16)
    x_lo = (x - x_hi.astype(F32)).astype(BF16)
    w_hi = w.astype(BF16)
    w_lo = (w - w_hi.astype(F32)).astype(BF16)
    return _dot(x_hi, w_hi) + _dot(x_lo, w_hi) + _dot(x_hi, w_lo)


def _gelu(x):
    return 0.5 * x * (1.0 + jnp.tanh(0.7978845608028654 * (x + 0.044715 * (x * x * x))))


def _sigmoid(x):
    return 1.0 / (1.0 + jnp.exp(-x))


def _log_sigmoid(x):
    return jnp.minimum(x, 0.0) - jnp.log1p(jnp.exp(-jnp.abs(x)))


def _layer_norm(y, g, b):
    mu = jnp.mean(y, axis=-1, keepdims=True)
    d = y - mu
    var = jnp.mean(d * d, axis=-1, keepdims=True)
    return d * lax.rsqrt(var + LN_EPS) * g + b


def _full(shape):
    n = len(shape)
    return pl.BlockSpec(shape, lambda *_: (0,) * n)


def _any():
    return pl.BlockSpec(memory_space=pl.ANY)


def _call_inplace(kernel, **kw):
    return pl.pallas_call(kernel, input_output_aliases={0: 0}, **kw)


def _proj_kernel(x_ref, w_ref, *o_refs):
    xb = x_ref[...].astype(BF16)
    n = o_refs[0].shape[-1]
    for j, o_ref in enumerate(o_refs):
        o_ref[...] = _dot(xb, w_ref[:, j * n:(j + 1) * n])


def _proj(x, w_bf16, n_out=1, tm=ROW_TM):
    T, K = x.shape
    N = w_bf16.shape[1]
    n = N // n_out
    tm = min(tm, T)
    return pl.pallas_call(
        _proj_kernel,
        grid=(T // tm,),
        in_specs=[pl.BlockSpec((tm, K), lambda i: (i, 0)), _full((K, N))],
        out_specs=[pl.BlockSpec((tm, n), lambda i: (i, 0)) for _ in range(n_out)],
        out_shape=[jax.ShapeDtypeStruct((T, n), F32) for _ in range(n_out)],
        compiler_params=_cparams(("parallel",)),
        name="proj",
    )(x, w_bf16)


def _ab_kernel(x_ref, conv0_ref, h0_ref, w_in_ref, lng_ref, lnb_ref, ws_ref, bst_ref,
               cw_ref, cb_ref, wa_ref, ba_ref, wx_ref, bx_ref, lam_ref, w_out_ref,
               g1_ref, b1_ref, x1_ref, *rest, tl, emit_v):
    if emit_v:
        v_ref, buf_ref, hl_ref, xp_s, h_s = rest
    else:
        buf_ref, hl_ref, xp_s, h_s = rest
        v_ref = None
    l = pl.program_id(1)

    @pl.when(l == 0)
    def _():
        xp_s[0:SUBLANES, :] = conv0_ref[...]
        h_s[...] = h0_ref[...]

    x = x_ref[...]
    proj = _dot(x.astype(BF16), w_in_ref[...])
    u = _gelu(proj[:, 0:A_WIDTH])
    v = _gelu(proj[:, A_WIDTH:2 * A_WIDTH])
    y_br = proj[:, 2 * A_WIDTH:2 * A_WIDTH + B_WIDTH]
    x_br = proj[:, 2 * A_WIDTH + B_WIDTH:]

    row = lax.broadcasted_iota(jnp.int32, (tl, tl), 0)
    col = lax.broadcasted_iota(jnp.int32, (tl, tl), 1)
    causal = col <= row
    out = DN_ALPHA * x
    for g in range(A_GROUPS):
        lo, hi = g * A_GROUP_DIM, (g + 1) * A_GROUP_DIM
        vn = _layer_norm(v[:, lo:hi], lng_ref[:, lo:hi], lnb_ref[:, lo:hi])
        if emit_v:
            v_ref[:, lo:hi] = vn
        wg = jnp.where(causal, ws_ref[g, 0:tl, 0:tl], 0.0).astype(BF16)
        z = _dot(wg, vn.astype(BF16)) + bst_ref[0:tl, g:g + 1]
        a_out = u[:, lo:hi] * z
        out = out + _dot(a_out.astype(BF16), w_out_ref[lo:hi, :])

    xp_s[SUBLANES:SUBLANES + tl, :] = x_br
    xc = (cb_ref[...] + cw_ref[3:4, :] * x_br
          + cw_ref[2:3, :] * xp_s[SUBLANES - 1:SUBLANES - 1 + tl, :]
          + cw_ref[1:2, :] * xp_s[SUBLANES - 2:SUBLANES - 2 + tl, :]
          + cw_ref[0:1, :] * xp_s[SUBLANES - 3:SUBLANES - 3 + tl, :])
    tail = xp_s[tl:tl + SUBLANES, :]
    buf_ref[...] = tail
    xp_s[0:SUBLANES, :] = tail

    xcb = xc.astype(BF16)
    r = _sigmoid(_dot(xcb, wa_ref[...]) + ba_ref[...])
    i = _sigmoid(_dot(xcb, wx_ref[...]) + bx_ref[...])
    lam = lam_ref[...]
    softplus_neg_lam = jnp.maximum(-lam, 0.0) + jnp.log1p(jnp.exp(-jnp.abs(lam)))
    log_a = -B_C * r * softplus_neg_lam
    a = jnp.exp(log_a)
    one_minus_a2 = -jnp.tanh(log_a) * (a * a + 1.0)
    bb = (xc * i) * jnp.sqrt(one_minus_a2)

    rows = lax.broadcasted_iota(jnp.int32, (tl, B_WIDTH), 0)
    k = 1
    while k < tl:
        a_sh = pltpu.roll(a, k, 0)
        b_sh = pltpu.roll(bb, k, 0)
        m = rows >= k
        bb = jnp.where(m, a * b_sh + bb, bb)
        a = jnp.where(m, a * a_sh, a)
        k *= 2
    h = a * h_s[...] + bb
    h_last = h[tl - 1:tl, :]
    h_s[...] = h_last
    hl_ref[...] = h_last
    b_out = _gelu(y_br) * h
    out = out + _dot(b_out.astype(BF16), w_out_ref[A_WIDTH:, :])
    x1_ref[...] = _layer_norm(out, g1_ref[...], b1_ref[...])


def _ab_layer(x_all, row_off, B, L, conv0, h0, w, emit_v):
    T = x_all.shape[0]
    tl = min(L, A_CHUNK)
    nl = L // tl
    off = row_off // tl
    rows = lambda n: pl.BlockSpec((tl, n), lambda b, l: (off + b * nl + l, 0))
    local = lambda n: pl.BlockSpec((tl, n), lambda b, l: (b * nl + l, 0))
    st_spec = lambda r: pl.BlockSpec((None, r, B_WIDTH), lambda b, l: (b, 0, 0))
    weights = [w["w_in"], w["ln_g"], w["ln_b"], w["w_s"], w["b_st"], w["conv_w"], w["conv_b"],
               w["wa"], w["ba"], w["wx"], w["bx"], w["lam"], w["w_out"], w["g1"], w["b1"]]
    in_specs = [rows(D_MODEL), st_spec(SUBLANES), st_spec(1)] + [_full(a.shape) for a in weights]
    out_specs = [rows(D_MODEL)] + ([local(A_WIDTH)] if emit_v else []) + [st_spec(SUBLANES), st_spec(1)]
    out_shape = ([jax.ShapeDtypeStruct((T, D_MODEL), F32)]
                 + ([jax.ShapeDtypeStruct((B * L, A_WIDTH), F32)] if emit_v else [])
                 + [jax.ShapeDtypeStruct((B, SUBLANES, B_WIDTH), F32),
                    jax.ShapeDtypeStruct((B, 1, B_WIDTH), F32)])
    call = _call_inplace(
        functools.partial(_ab_kernel, tl=tl, emit_v=emit_v),
        grid=(B, nl), in_specs=in_specs, out_specs=out_specs, out_shape=out_shape,
        scratch_shapes=[pltpu.VMEM((tl + SUBLANES, B_WIDTH), F32), pltpu.VMEM((1, B_WIDTH), F32)],
        compiler_params=_cparams(("parallel", "arbitrary")), name="ab_mixer")
    return call(x_all, conv0, h0, *weights)


def _out_ln_kernel(x_ref, o_ref, w_ref, g_ref, b_ref, y_ref):
    mix = _dot(o_ref[...].astype(BF16), w_ref[...])
    y_ref[...] = _layer_norm(DN_ALPHA * x_ref[...] + mix, g_ref[...], b_ref[...])


def _out_ln(x_all, row_off, o, w_bf16, g, b, tm=ROW_TM):
    T = x_all.shape[0]
    n, K = o.shape
    tm = min(tm, n)
    off = row_off // tm
    rows = pl.BlockSpec((tm, D_MODEL), lambda i: (off + i, 0))
    in_specs = [rows, pl.BlockSpec((tm, K), lambda i: (i, 0)), _full(w_bf16.shape), _full(g.shape), _full(b.shape)]
    call = _call_inplace(
        _out_ln_kernel,
        grid=(n // tm,), in_specs=in_specs, out_specs=rows,
        out_shape=jax.ShapeDtypeStruct((T, D_MODEL), F32),
        compiler_params=_cparams(("parallel",)), name="out_ln")
    return call(x_all, o, w_bf16, g, b)


def _fox_pre_p_kernel(x_ref, wkt_ref, wvt_ref, wqa_ref, wka_ref, wf_ref, bf_ref, pq_ref, pk_ref,
                      kt_ref, vt_ref, lft_ref, qa_ref, ka_ref, carry_s, *, tm):
    @pl.when(pl.program_id(1) == 0)
    def _():
        carry_s[...] = jnp.zeros(carry_s.shape, F32)

    x = x_ref[...]
    xb = x.astype(BF16)
    kt_ref[...] = _dot_nt(wkt_ref[...], xb)
    vt_ref[...] = _dot_nt(wvt_ref[...], xb)

    f = _dot_x3(x, wf_ref[...]) + bf_ref[...]
    lane = lax.broadcasted_iota(jnp.int32, (tm, LANES), 1)
    logf = jnp.where(lane < C_HEADS, _log_sigmoid(f), 0.0)
    lft_ref[...] = logf.T[0:C_HEADS, :]

    c = logf
    rows = lax.broadcasted_iota(jnp.int32, (tm, LANES), 0)
    k = 1
    while k < tm:
        c = c + jnp.where(rows >= k, pltpu.roll(c, k, 0), 0.0)
        k *= 2
    c = c + carry_s[...]
    carry_s[...] = c[tm - 1:tm, :]

    hi = c.astype(BF16).astype(F32)
    r1 = c - hi
    mid = r1.astype(BF16).astype(F32)
    lo = (r1 - mid).astype(BF16).astype(F32)
    pieces = (hi + pltpu.roll(mid, C_HEADS, 1) + pltpu.roll(lo, 2 * C_HEADS, 1)
              + jnp.where(lane == ONE_LANE, 1.0, 0.0)).astype(BF16)
    first = lane < C_HEAD_DIM

    def to_slots(v):
        out = []
        for pair in range(C_HEADS // 2):
            blk = v[:, pair * LANES:(pair + 1) * LANES]
            out.append(jnp.where(first, blk, 0.0))
            out.append(jnp.where(first, pltpu.roll(blk, C_HEAD_DIM, 1), 0.0))
        return jnp.concatenate(out, axis=1)

    qa_ref[...] = (to_slots(_dot(xb, wqa_ref[...])) + _dot(pieces, pq_ref[...])).astype(BF16)
    ka_ref[...] = (to_slots(_dot(xb, wka_ref[...])) + _dot(pieces, pk_ref[...])).astype(BF16)


def _piece_placement():
    pq = np.zeros((LANES, QA_W), np.float32)
    pk = np.zeros((LANES, QA_W), np.float32)
    for h in range(C_HEADS):
        base = h * SLOT + C_HEAD_DIM
        for p in range(N_PIECE):
            pq[p * C_HEADS + h, base + p] = 1.0
            pq[ONE_LANE, base + N_PIECE + p] = 1.0
            pk[ONE_LANE, base + p] = 1.0
            pk[p * C_HEADS + h, base + N_PIECE + p] = -1.0
    return jnp.asarray(pq, BF16), jnp.asarray(pk, BF16)


def _fox_pre_p(x_all, B, L, w, j, kt_prev, vt_prev, lft_prev, tm=FOX_TM):
    nl = L // tm
    weights = [w["wkt"], w["wvt"], w["wqa"], w["wka"], w["w_f"], w["b_f"], w["pq"], w["pk"]]
    in_specs = [pl.BlockSpec((tm, D_MODEL), lambda b, l: (b * nl + l, 0))] + [_full(a.shape) for a in weights]
    t_spec = lambda n: pl.BlockSpec((None, None, n, tm), lambda b, l: (j, b, 0, l))
    a_spec = pl.BlockSpec((tm, QA_W), lambda b, l: (b * nl + l, 0))
    out_specs = [t_spec(D_MODEL), t_spec(D_MODEL), t_spec(C_HEADS), a_spec, a_spec]
    out_shape = [jax.ShapeDtypeStruct((N_ODD, B, D_MODEL, L), F32), jax.ShapeDtypeStruct((N_ODD, B, D_MODEL, L), F32),
                 jax.ShapeDtypeStruct((N_ODD, B, C_HEADS, L), F32),
                 jax.ShapeDtypeStruct((B * L, QA_W), BF16), jax.ShapeDtypeStruct((B * L, QA_W), BF16)]
    kern = functools.partial(_fox_pre_p_kernel, tm=tm)
    kw = dict(grid=(B, nl), out_specs=out_specs, out_shape=out_shape,
              scratch_shapes=[pltpu.VMEM((1, LANES), F32)],
              compiler_params=_cparams(("parallel", "arbitrary")), name="fox_pre_p")
    n_in = len(in_specs)

    def with_prev(*refs):
        return kern(*refs[:n_in], *refs[n_in + 3:])
    return pl.pallas_call(with_prev, in_specs=in_specs + [_any()] * 3,
                          input_output_aliases={n_in: 0, n_in + 1: 1, n_in + 2: 2}, **kw)(
        x_all, *weights, kt_prev, vt_prev, lft_prev)


def _fox_attn_p_kernel(qa_ref, ka_ref, vt_ref, o_ref, acc_s, *, t, tk):
    qi = pl.program_id(2)
    per = t // tk
    qas = [qa_ref[:, hh * SLOT:(hh + 1) * SLOT] for hh in range(2)]
    acc_s[...] = jnp.zeros(acc_s.shape, F32)
    krow = lax.broadcasted_iota(jnp.int32, (tk, t), 0)
    qcol = lax.broadcasted_iota(jnp.int32, (tk, t), 1)

    def scores(off):
        return tuple(_dot_nt(ka_ref[pl.ds(off, tk), hh * SLOT:(hh + 1) * SLOT], qas[hh]) for hh in range(2))

    def consume(off, ss, carry, diag):
        new = []
        for hh in range(2):
            m_old, l_old = carry[2 * hh], carry[2 * hh + 1]
            s = ss[hh]
            if diag is not None:
                s = jnp.where(krow + diag * tk <= qcol, s, NEG_INF)
            m_new = jnp.maximum(m_old, jnp.max(s, axis=0, keepdims=True))
            p = jnp.exp(s - m_new)
            alpha = jnp.exp(m_old - m_new)
            l_new = alpha * l_old + jnp.sum(p, axis=0, keepdims=True)
            vc = vt_ref[hh * C_HEAD_DIM:(hh + 1) * C_HEAD_DIM, pl.ds(off, tk)].astype(BF16)
            acc_s[hh] = alpha * acc_s[hh] + _dot(vc, p.astype(BF16))
            new += [m_new, l_new]
        return tuple(new)

    def body(jj, c):
        nxt = scores(pl.multiple_of((jj + 1) * tk, tk))
        return consume(pl.multiple_of(jj * tk, tk), c[4:], c[:4], None) + nxt

    init = (jnp.full((1, t), NEG_INF, F32), jnp.zeros((1, t), F32)) * 2
    c = lax.fori_loop(0, qi * per, body, init + scores(0))
    for d in range(per):
        off = pl.multiple_of(qi * t + d * tk, tk)
        nxt = scores(pl.multiple_of(qi * t + (d + 1) * tk, tk)) if d + 1 < per else ()
        c = consume(off, c[4:], c[:4], d) + nxt
    o_t = jnp.concatenate([acc_s[0] / c[1], acc_s[1] / c[3]], axis=0)
    o_ref[...] = o_t.T


def _fox_attn_p(qa, ka, vt, j, B, L, t=ATT_T, tk=ATT_TK):
    nq = L // t
    n_hp = C_HEADS // 2
    return pl.pallas_call(
        functools.partial(_fox_attn_p_kernel, t=t, tk=tk),
        grid=(B, n_hp, nq),
        in_specs=[pl.BlockSpec((t, 2 * SLOT), lambda b, h, i: (b * nq + i, h)),
                  pl.BlockSpec((L, 2 * SLOT), lambda b, h, i: (b, h)),
                  pl.BlockSpec((None, None, 2 * C_HEAD_DIM, L), lambda b, h, i: (j, b, h, 0))],
        out_specs=pl.BlockSpec((t, LANES), lambda b, h, i: (b * nq + i, h)),
        out_shape=jax.ShapeDtypeStruct((B * L, D_MODEL), F32),
        scratch_shapes=[pltpu.VMEM((2, C_HEAD_DIM, t), F32)],
        compiler_params=_cparams(("parallel", "parallel", "arbitrary")),
        name="fox_attn_p",
    )(qa, ka, vt)


def _fox_pre_s_kernel(x_ref, w_ref, wf_ref, bf_ref, q_ref, k_ref, v_ref, lf_ref):
    x = x_ref[...]
    xb = x.astype(BF16)
    q_ref[...] = _dot(xb, w_ref[:, 0:D_MODEL])
    k_ref[...] = _dot(xb, w_ref[:, D_MODEL:2 * D_MODEL])
    v_ref[...] = _dot(xb, w_ref[:, 2 * D_MODEL:3 * D_MODEL])
    f = _dot_x3(x, wf_ref[...]) + bf_ref[...]
    lf_ref[...] = _log_sigmoid(f)[:, 0:C_HEADS]


def _fox_pre_s(x_all, row_off, n, w_qkv, w_f, b_f, tm=ROW_TM):
    tm = min(tm, n)
    off = row_off // tm
    row = lambda c: pl.BlockSpec((tm, c), lambda i: (i, 0))
    return pl.pallas_call(
        _fox_pre_s_kernel,
        grid=(n // tm,),
        in_specs=[pl.BlockSpec((tm, D_MODEL), lambda i: (off + i, 0)),
                  _full(w_qkv.shape), _full(w_f.shape), _full(b_f.shape)],
        out_specs=[row(D_MODEL), row(D_MODEL), row(D_MODEL), row(C_HEADS)],
        out_shape=[jax.ShapeDtypeStruct((n, D_MODEL), F32)] * 3 + [jax.ShapeDtypeStruct((n, C_HEADS), F32)],
        compiler_params=_cparams(("parallel",)),
        name="fox_pre_s",
    )(x_all, w_qkv, w_f, b_f)


def _cumsum_kernel(x_ref, o_ref):
    x = x_ref[...]
    n = x.shape[-1]
    lane = lax.broadcasted_iota(jnp.int32, x.shape, 1)
    k = 1
    while k < n:
        x = x + jnp.where(lane >= k, pltpu.roll(x, k, 1), 0.0)
        k *= 2
    o_ref[...] = x


def _cumsum_lanes(x):
    B, H, Lp = x.shape
    spec = pl.BlockSpec((None, H, Lp), lambda b: (b, 0, 0))
    return pl.pallas_call(
        _cumsum_kernel, grid=(B,), in_specs=[spec], out_specs=spec,
        out_shape=jax.ShapeDtypeStruct(x.shape, F32),
        compiler_params=_cparams(("parallel",)), name="logf_cumsum",
    )(x)


def _fox_attn_s_kernel(q_ref, kn_ref, vn_ref, kt_ref, vt_ref, cq_ref, ck_ref, o_ref, *, L, P):
    row = lax.broadcasted_iota(jnp.int32, (2 * L, LANES), 0)
    lane = lax.broadcasted_iota(jnp.int32, (2 * L, LANES), 1)
    q = q_ref[...] * C_SCALE
    q2 = jnp.concatenate([q, q], axis=0)
    qb = jnp.where((row < L) == (lane < C_HEAD_DIM), q2, 0.0).astype(BF16)
    s_p = _dot(qb, kt_ref[...].astype(BF16))
    s_n = _dot_nt(qb, kn_ref[...].astype(BF16))
    cq2 = jnp.concatenate([cq_ref[0], cq_ref[1]], axis=0)
    first_p = lax.broadcasted_iota(jnp.int32, (2 * L, P), 0) < L
    s_p = s_p + (cq2 - jnp.where(first_p, ck_ref[0:1, 0:P], ck_ref[1:2, 0:P]))
    rn = lax.broadcasted_iota(jnp.int32, (2 * L, L), 0)
    cn = lax.broadcasted_iota(jnp.int32, (2 * L, L), 1)
    s_n = s_n + (cq2 - jnp.where(rn < L, ck_ref[0:1, P:P + L], ck_ref[1:2, P:P + L]))
    s_n = jnp.where(cn <= jnp.where(rn < L, rn, rn - L), s_n, NEG_INF)
    m = jnp.maximum(jnp.max(s_p, axis=-1, keepdims=True), jnp.max(s_n, axis=-1, keepdims=True))
    p_p = jnp.exp(s_p - m)
    p_n = jnp.exp(s_n - m)
    den = jnp.sum(p_p, axis=-1, keepdims=True) + jnp.sum(p_n, axis=-1, keepdims=True)
    o2 = (_dot_nt(p_p.astype(BF16), vt_ref[...].astype(BF16))
          + _dot(p_n.astype(BF16), vn_ref[...].astype(BF16))) / den
    first_lanes = lax.broadcasted_iota(jnp.int32, (L, LANES), 1) < C_HEAD_DIM
    o_ref[...] = jnp.where(first_lanes, o2[0:L], o2[L:2 * L])


def _fox_attn_s(q, k, v, kt_cache, vt_cache, j, cq, ck, B, L):
    P = kt_cache.shape[-1]
    n_hp = C_HEADS // 2
    row = pl.BlockSpec((L, LANES), lambda b, h: (b, h))
    cache = pl.BlockSpec((None, None, 2 * C_HEAD_DIM, P), lambda b, h: (j, b, h, 0))
    return pl.pallas_call(
        functools.partial(_fox_attn_s_kernel, L=L, P=P),
        grid=(B, n_hp),
        in_specs=[row, row, row, cache, cache,
                  pl.BlockSpec((None, None, 2, L, 1), lambda b, h: (b, h, 0, 0, 0)),
                  pl.BlockSpec((None, None, 2, ck.shape[-1]), lambda b, h: (b, h, 0, 0))],
        out_specs=row,
        out_shape=jax.ShapeDtypeStruct((B * L, D_MODEL), F32),
        compiler_params=_cparams(("parallel", "parallel")),
        name="fox_attn_s",
    )(q, k, v, kt_cache, vt_cache, cq, ck)


def _mem_kernel(x_ref, mk_ref, mv_ref, wq_ref, wo_ref, g_ref, b_ref, y_ref):
    x = x_ref[...]
    q = _dot(x.astype(BF16), wq_ref[...]).astype(BF16)
    mv_t = mv_ref[...].T.astype(BF16)
    o_t = []
    for h in range(M_HEADS):
        lo, hi = h * M_HEAD_DIM, (h + 1) * M_HEAD_DIM
        s = _dot_nt(mk_ref[:, lo:hi].astype(BF16), q[:, lo:hi]) * M_SCALE
        p = jnp.exp(s - jnp.max(s, axis=0, keepdims=True))
        den = jnp.sum(p, axis=0, keepdims=True)
        o_t.append(_dot(mv_t[lo:hi, :], p.astype(BF16)) / den)
    o = jnp.concatenate(o_t, axis=0).T
    out = DN_ALPHA * x + _dot(o.astype(BF16), wo_ref[...])
    y_ref[...] = _layer_norm(out, g_ref[...], b_ref[...])


def _mem_layer(x_all, row_off, B, L, mk, mv, layer, w_q, w_o, g, b, tl=256):
    T = x_all.shape[0]
    tl = min(tl, L)
    nl = L // tl
    off = row_off // tl
    rows = pl.BlockSpec((tl, D_MODEL), lambda bb, l: (off + bb * nl + l, 0))
    mem = pl.BlockSpec((None, None, N_MEM, M_WIDTH), lambda bb, l: (layer, bb, 0, 0))
    in_specs = [rows, mem, mem, _full(w_q.shape), _full(w_o.shape), _full(g.shape), _full(b.shape)]
    call = _call_inplace(
        _mem_kernel,
        grid=(B, nl), in_specs=in_specs, out_specs=rows,
        out_shape=jax.ShapeDtypeStruct((T, D_MODEL), F32),
        compiler_params=_cparams(("parallel", "parallel")), name="mem_attend")
    return call(x_all, mk, mv, w_q, w_o, g, b)


def _router_kernel(x_ref, wr_ref, br_ref, mi_ref, mw_ref, cnt_ref, carry_s, *, tm):
    @pl.when(pl.program_id(0) == 0)
    def _():
        carry_s[...] = jnp.zeros(carry_s.shape, F32)

    x = x_ref[...]
    logits = _dot_x3(x, wr_ref[...]) + br_ref[...]
    lane = lax.broadcasted_iota(jnp.int32, (tm, LANES), 1)
    work = jnp.where(lane < N_EXPERTS, logits, NEG_INF)
    vals, idxs = [], []
    for _ in range(TOP_K):
        mx = jnp.max(work, axis=-1, keepdims=True)
        idx = jnp.min(jnp.where(work == mx, lane, LANES), axis=-1, keepdims=True)
        vals.append(mx)
        idxs.append(idx)
        work = jnp.where(lane == idx, NEG_INF, work)
    es = [jnp.exp(vv - vals[0]) for vv in vals]
    den = es[0] + es[1] + es[2] + es[3]
    sel = jnp.zeros((tm, LANES), F32)
    for idx in idxs:
        sel = jnp.where(lane == idx, 1.0, sel)
    rr = lax.broadcasted_iota(jnp.int32, (tm, tm), 0)
    cc = lax.broadcasted_iota(jnp.int32, (tm, tm), 1)
    strict = jnp.where(cc < rr, 1.0, 0.0).astype(BF16)
    rank = carry_s[...] + _dot(strict, sel.astype(BF16))
    mi = jnp.zeros((tm, LANES), jnp.int32)
    mw = jnp.zeros((tm, LANES), F32)
    for kk in range(TOP_K):
        rk = jnp.sum(jnp.where(lane == idxs[kk], rank, 0.0), axis=-1, keepdims=True)
        mi = jnp.where(lane == kk, idxs[kk], mi)
        mi = jnp.where(lane == TOP_K + kk, rk.astype(jnp.int32), mi)
        mw = jnp.where(lane == kk, es[kk] / den, mw)
    mi_ref[...] = mi
    mw_ref[...] = mw
    total = carry_s[...] + jnp.sum(sel, axis=0, keepdims=True)
    carry_s[...] = total
    cnt_ref[...] = total


def _router(x, w_r, b_r, tm=ROW_TM):
    T = x.shape[0]
    tm = min(tm, T)
    row = lambda: pl.BlockSpec((tm, LANES), lambda i: (i, 0))
    return pl.pallas_call(
        functools.partial(_router_kernel, tm=tm),
        grid=(T // tm,),
        in_specs=[pl.BlockSpec((tm, D_MODEL), lambda i: (i, 0)), _full(w_r.shape), _full(b_r.shape)],
        out_specs=[row(), row(), _full((1, LANES))],
        out_shape=[jax.ShapeDtypeStruct((T, LANES), jnp.int32), jax.ShapeDtypeStruct((T, LANES), F32),
                   jax.ShapeDtypeStruct((1, LANES), F32)],
        scratch_shapes=[pltpu.VMEM((1, LANES), F32)],
        compiler_params=_cparams(("arbitrary",)),
        name="router",
    )(x, w_r, b_r)


def _row_copy(src, src_row, dst, dst_row, sem):
    return pltpu.make_async_copy(src.at[pl.ds(src_row, 1)], dst.at[pl.ds(dst_row, 1)], sem)


def _dispatch_kernel(pos_ref, x_ref, xs_in_ref, xs_ref, sem, *, tm):
    del xs_in_ref
    base = pl.program_id(0) * tm * TOP_K

    def issue(t, c):
        for kk in range(TOP_K):
            _row_copy(x_ref, t, xs_ref, pos_ref[base + t * TOP_K + kk], sem).start()
        return c
    lax.fori_loop(0, tm, issue, 0)

    def drain(t, c):
        for kk in range(TOP_K):
            _row_copy(x_ref, 0, xs_ref, 0, sem).wait()
        return c
    lax.fori_loop(0, tm, drain, 0)


def _dispatch(pos_flat, x, xs_buf, tm=DISP_TM):
    T = x.shape[0]
    tm = min(tm, T)
    return pl.pallas_call(
        functools.partial(_dispatch_kernel, tm=tm),
        grid_spec=pltpu.PrefetchScalarGridSpec(
            num_scalar_prefetch=1,
            grid=(T // tm,),
            in_specs=[pl.BlockSpec((tm, D_MODEL), lambda i, pos: (i, 0)), _any()],
            out_specs=_any(),
            scratch_shapes=[pltpu.SemaphoreType.DMA(())],
        ),
        out_shape=jax.ShapeDtypeStruct(xs_buf.shape, F32),
        input_output_aliases={2: 0},
        compiler_params=_cparams(("arbitrary",)),
        name="moe_dispatch",
    )(pos_flat, x, xs_buf)


def _gmm_kernel(te_ref, nu_ref, xs_ref, wgu_ref, bgu_ref, wdn_ref, bdn_ref, ys_ref, wgu_s, wdn_s):
    i = pl.program_id(0)

    @pl.when((i == 0) | (te_ref[i] != te_ref[jnp.maximum(i - 1, 0)]))
    def _():
        def cast_rows(c, carry):
            r = pl.multiple_of(c * LANES, LANES)
            wgu_s[pl.ds(r, LANES), :] = wgu_ref[pl.ds(r, LANES), :].astype(BF16)
            wdn_s[pl.ds(r, LANES), :] = wdn_ref[pl.ds(r, LANES), :].astype(BF16)
            return carry
        lax.fori_loop(0, D_MODEL // LANES, cast_rows, 0)

    @pl.when(i < nu_ref[0])
    def _():
        xb = xs_ref[...].astype(BF16)
        gu = _dot(xb, wgu_s[...]) + bgu_ref[...]
        g = jnp.minimum(gu[:, 0:D_EXPERT], SWIGLU_LIMIT)
        u = jnp.clip(gu[:, D_EXPERT:], -SWIGLU_LIMIT, SWIGLU_LIMIT)
        h = (u + 1.0) * g * _sigmoid(SWIGLU_ALPHA * g)
        ys_ref[...] = _dot(h.astype(BF16), wdn_s[...]) + bdn_ref[...]

    @pl.when(i >= nu_ref[0])
    def _():
        ys_ref[...] = jnp.zeros(ys_ref.shape, F32)


def _gmm(tile_expert, n_used, xs, w_gu, b_gu, w_dn, b_dn, layer, tm=MOE_TM):
    R = xs.shape[0]
    wmap = lambda i, te, nu: (layer, te[i], 0, 0)
    return pl.pallas_call(
        _gmm_kernel,
        grid_spec=pltpu.PrefetchScalarGridSpec(
            num_scalar_prefetch=2,
            grid=(R // tm,),
            in_specs=[pl.BlockSpec((tm, D_MODEL), lambda i, te, nu: (i, 0)),
                      pl.BlockSpec((None, None, D_MODEL, 2 * D_EXPERT), wmap),
                      pl.BlockSpec((None, None, 1, 2 * D_EXPERT), wmap),
                      pl.BlockSpec((None, None, D_EXPERT, D_MODEL), wmap),
                      pl.BlockSpec((None, None, 1, D_MODEL), wmap)],
            out_specs=pl.BlockSpec((tm, D_MODEL), lambda i, te, nu: (i, 0)),
            scratch_shapes=[pltpu.VMEM((D_MODEL, 2 * D_EXPERT), BF16), pltpu.VMEM((D_EXPERT, D_MODEL), BF16)],
        ),
        out_shape=jax.ShapeDtypeStruct((R, D_MODEL), F32),
        compiler_params=_cparams(("arbitrary",)),
        name="moe_gmm",
    )(tile_expert, n_used, xs, w_gu, b_gu, w_dn, b_dn)


def _combine_kernel(pos_ref, x_ref, mw_ref, ys_ref, g_ref, b_ref, y_ref, buf, sem, *, tm):
    base = pl.program_id(0) * tm * TOP_K

    def issue(t, c):
        for kk in range(TOP_K):
            _row_copy(ys_ref, pos_ref[base + t * TOP_K + kk], buf.at[kk], t, sem).start()
        return c
    lax.fori_loop(0, tm, issue, 0)

    def drain(t, c):
        for kk in range(TOP_K):
            _row_copy(ys_ref, 0, buf.at[kk], 0, sem).wait()
        return c
    lax.fori_loop(0, tm, drain, 0)

    out = DN_ALPHA * x_ref[...]
    for kk in range(TOP_K):
        out = out + mw_ref[:, kk:kk + 1] * buf[kk]
    y_ref[...] = _layer_norm(out, g_ref[...], b_ref[...])


def _combine(pos_flat, x, meta_w, ys, g, b, tm=DISP_TM):
    T = x.shape[0]
    tm = min(tm, T)
    return pl.pallas_call(
        functools.partial(_combine_kernel, tm=tm),
        grid_spec=pltpu.PrefetchScalarGridSpec(
            num_scalar_prefetch=1,
            grid=(T // tm,),
            in_specs=[pl.BlockSpec((tm, D_MODEL), lambda i, pos: (i, 0)),
                      pl.BlockSpec((tm, LANES), lambda i, pos: (i, 0)),
                      _any(),
                      pl.BlockSpec((1, D_MODEL), lambda i, pos: (0, 0)),
                      pl.BlockSpec((1, D_MODEL), lambda i, pos: (0, 0))],
            out_specs=pl.BlockSpec((tm, D_MODEL), lambda i, pos: (i, 0)),
            scratch_shapes=[pltpu.VMEM((TOP_K, tm, D_MODEL), F32), pltpu.SemaphoreType.DMA(())],
        ),
        out_shape=jax.ShapeDtypeStruct((T, D_MODEL), F32),
        compiler_params=_cparams(("arbitrary",)),
        name="moe_combine",
    )(pos_flat, x, meta_w, ys, g, b)


def _moe_layer(x, xs_buf, w, layer, tm=MOE_TM):
    T = x.shape[0]
    R = xs_buf.shape[0]
    n_tiles = R // tm
    meta_i, meta_w, counts = _router(x, w["w_r"], w["b_r"])
    cnt = counts[0, :N_EXPERTS].astype(jnp.int32)
    tiles_per = (cnt + tm - 1) // tm
    tile_end = jnp.cumsum(tiles_per)
    group_start = (tile_end - tiles_per) * tm
    n_used = tile_end[-1:]
    tile_ids = jnp.arange(n_tiles, dtype=jnp.int32)
    tile_expert = jnp.sum(tile_ids[:, None] >= tile_end[None, :], axis=1).astype(jnp.int32)
    last_expert = jnp.sum(jnp.maximum(n_used - 1, 0) >= tile_end).astype(jnp.int32)
    tile_expert = jnp.where(tile_ids < n_used, tile_expert, last_expert)
    pos = group_start[meta_i[:, :TOP_K]] + meta_i[:, TOP_K:2 * TOP_K]
    pos_flat = pos.reshape(T * TOP_K).astype(jnp.int32)
    xs = _dispatch(pos_flat, x, xs_buf)
    ys = _gmm(tile_expert, n_used.astype(jnp.int32), xs, w["w_gu"], w["b_gu"], w["w_dn"], w["b_dn"], layer, tm)
    return _combine(pos_flat, x, meta_w, ys, w["g3"], w["b3"]), xs


def _block_diag(w):
    eye = jnp.eye(B_BLOCKS, dtype=w.dtype)
    return jnp.einsum("hij,hg->higj", w, eye).reshape(B_WIDTH, B_WIDTH)


def _pad_lanes(a, value=0.0):
    return jnp.pad(a, ((0, 0),) * (a.ndim - 1) + ((0, LANES - a.shape[-1]),), constant_values=value)


def kernel(x_prompt, x_sample, state_b_conv, state_b_h, cache_c_k, cache_c_v, cache_c_logf, cache_mem_k, cache_mem_v, mem_prompt, w_in_ab, a_ln_g, a_ln_b, a_w_s, a_b_s, b_conv_w, b_conv_b, b_w_a, b_b_a, b_w_x, b_b_x, b_lambda, w_out_ab, c_w_in, c_b_f, c_w_out, w_mq, w_mk, w_mv, w_mo, w_router, b_router, w_gu, b_gu, w_dn, b_dn, ln1_g, ln1_b, ln2_g, ln2_b, ln3_g, ln3_b):
    Bp, Lp, _ = x_prompt.shape
    Bs, Ls, _ = x_sample.shape
    P = cache_c_k.shape[2]
    Tp, Ts = Bp * Lp, Bs * Ls
    T = Tp + Ts
    row = lambda a: a.reshape(1, -1)

    mem_flat = mem_prompt.reshape(Bp * N_MEM, D_MODEL)
    w_mkv = jnp.concatenate([w_mk, w_mv], axis=-1).astype(BF16)
    p_mem = [_proj(mem_flat, w_mkv[l], n_out=2) for l in range(DEPTH)]
    p_mem_k = jnp.stack([p[0] for p in p_mem]).reshape(DEPTH, Bp, N_MEM, M_WIDTH)
    p_mem_v = jnp.stack([p[1] for p in p_mem]).reshape(DEPTH, Bp, N_MEM, M_WIDTH)
    s_mem_k = cache_mem_k.reshape(DEPTH, Bs, N_MEM, M_WIDTH)
    s_mem_v = cache_mem_v.reshape(DEPTH, Bs, N_MEM, M_WIDTH)

    kt_cache = jnp.transpose(cache_c_k, (0, 1, 3, 4, 2)).reshape(N_ODD, Bs, D_MODEL, P)
    vt_cache = jnp.transpose(cache_c_v, (0, 1, 3, 4, 2)).reshape(N_ODD, Bs, D_MODEL, P)
    lft_cache = jnp.transpose(cache_c_logf, (0, 1, 3, 2))
    pq, pk = _piece_placement()

    b_gu4 = b_gu[:, :, None, :]
    b_dn4 = b_dn[:, :, None, :]
    R = T * TOP_K + N_EXPERTS * MOE_TM
    xs_buf = jnp.zeros((R, D_MODEL), F32)

    x_all = jnp.concatenate([x_prompt.reshape(Tp, D_MODEL), x_sample.reshape(Ts, D_MODEL)], axis=0)
    kt_p = jnp.zeros((N_ODD, Bp, D_MODEL, Lp), F32)
    vt_p = jnp.zeros((N_ODD, Bp, D_MODEL, Lp), F32)
    lft_p = jnp.zeros((N_ODD, Bp, C_HEADS, Lp), F32)
    st_p = {"b_conv": [], "b_h": []}
    st_s = {"a_v": [], "b_conv": [], "b_h": [], "c_k": [], "c_v": [], "c_logf": []}
    for layer in range(DEPTH):
        j = layer // 2
        g1, b1 = row(ln1_g[layer]), row(ln1_b[layer])
        if layer % 2 == 0:
            w = dict(w_in=w_in_ab[j].astype(BF16), ln_g=row(a_ln_g[j]), ln_b=row(a_ln_b[j]), w_s=a_w_s[j],
                     b_st=a_b_s[j].T, conv_w=b_conv_w[j], conv_b=row(b_conv_b[j]),
                     wa=_block_diag(b_w_a[j]).astype(BF16), ba=row(b_b_a[j]),
                     wx=_block_diag(b_w_x[j]).astype(BF16), bx=row(b_b_x[j]), lam=row(b_lambda[j]),
                     w_out=w_out_ab[j].astype(BF16), g1=g1, b1=b1)
            conv0_p = jnp.zeros((Bp, SUBLANES, B_WIDTH), F32)
            h0_p = jnp.zeros((Bp, 1, B_WIDTH), F32)
            x_all, buf_p, hl_p = _ab_layer(x_all, 0, Bp, Lp, conv0_p, h0_p, w, False)
            conv0_s = jnp.pad(state_b_conv[j], ((0, 0), (SUBLANES - (B_CONV - 1), 0), (0, 0)))
            x_all, v_s, buf_s, hl_s = _ab_layer(x_all, Tp, Bs, Ls, conv0_s, state_b_h[j][:, None, :], w, True)
            st_p["b_conv"].append(buf_p[:, SUBLANES - (B_CONV - 1):])
            st_p["b_h"].append(hl_p[:, 0])
            st_s["a_v"].append(v_s.reshape(Bs, Ls, A_WIDTH))
            st_s["b_conv"].append(buf_s[:, SUBLANES - (B_CONV - 1):])
            st_s["b_h"].append(hl_s[:, 0])
        else:
            wq, wk, wv = (c_w_in[j][:, i * D_MODEL:(i + 1) * D_MODEL] for i in range(3))
            w_f = _pad_lanes(c_w_in[j][:, 3 * D_MODEL:])
            b_f = _pad_lanes(row(c_b_f[j]))
            w_out = c_w_out[j].astype(BF16)
            wp = dict(wkt=wk.T.astype(BF16), wvt=wv.T.astype(BF16), wqa=(wq * C_SCALE).astype(BF16),
                      wka=wk.astype(BF16), w_f=w_f, b_f=b_f, pq=pq, pk=pk)
            kt_p, vt_p, lft_p, qa, ka = _fox_pre_p(x_all, Bp, Lp, wp, j, kt_p, vt_p, lft_p)
            o_p = _fox_attn_p(qa, ka, vt_p, j, Bp, Lp)
            q_s, k_s, v_s2, lf_s = _fox_pre_s(x_all, Tp, Ts, c_w_in[j][:, :3 * D_MODEL].astype(BF16), w_f, b_f)
            tot = P + Ls
            tot_pad = -(-tot // LANES) * LANES
            lf_t = jnp.concatenate([lft_cache[j], lf_s.reshape(Bs, Ls, C_HEADS).transpose(0, 2, 1),
                                    jnp.zeros((Bs, C_HEADS, tot_pad - tot), F32)], axis=2)
            cum = _cumsum_lanes(lf_t)
            ck = cum.reshape(Bs, C_HEADS // 2, 2, tot_pad)
            cq = cum[:, :, P:P + Ls].reshape(Bs, C_HEADS // 2, 2, Ls, 1)
            o_s = _fox_attn_s(q_s, k_s, v_s2, kt_cache, vt_cache, j, cq, ck, Bs, Ls)
            x_all = _out_ln(x_all, 0, o_p, w_out, g1, b1)
            x_all = _out_ln(x_all, Tp, o_s, w_out, g1, b1)
            st_s["c_k"].append(k_s.reshape(Bs, Ls, C_HEADS, C_HEAD_DIM))
            st_s["c_v"].append(v_s2.reshape(Bs, Ls, C_HEADS, C_HEAD_DIM))
            st_s["c_logf"].append(lf_s.reshape(Bs, Ls, C_HEADS))
        g2, b2 = row(ln2_g[layer]), row(ln2_b[layer])
        wq_m, wo_m = w_mq[layer].astype(BF16), w_mo[layer].astype(BF16)
        x_all = _mem_layer(x_all, 0, Bp, Lp, p_mem_k, p_mem_v, layer, wq_m, wo_m, g2, b2)
        x_all = _mem_layer(x_all, Tp, Bs, Ls, s_mem_k, s_mem_v, layer, wq_m, wo_m, g2, b2)
        wm = dict(w_r=_pad_lanes(w_router[layer]), b_r=_pad_lanes(row(b_router[layer])),
                  w_gu=w_gu, b_gu=b_gu4, w_dn=w_dn, b_dn=b_dn4,
                  g3=row(ln3_g[layer]), b3=row(ln3_b[layer]))
        x_all, xs_buf = _moe_layer(x_all, xs_buf, wm, layer)

    p_c_k = jnp.transpose(kt_p.reshape(N_ODD, Bp, C_HEADS, C_HEAD_DIM, Lp), (0, 1, 4, 2, 3))
    p_c_v = jnp.transpose(vt_p.reshape(N_ODD, Bp, C_HEADS, C_HEAD_DIM, Lp), (0, 1, 4, 2, 3))
    p_c_logf = jnp.transpose(lft_p, (0, 1, 3, 2))
    return (x_all[:Tp].reshape(Bp, Lp, D_MODEL), x_all[Tp:].reshape(Bs, Ls, D_MODEL),
            jnp.stack(st_p["b_conv"]), jnp.stack(st_p["b_h"]),
            p_c_k, p_c_v, p_c_logf,
            p_mem_k.reshape(DEPTH, Bp, N_MEM, M_HEADS, M_HEAD_DIM),
            p_mem_v.reshape(DEPTH, Bp, N_MEM, M_HEADS, M_HEAD_DIM),
            jnp.stack(st_s["a_v"]), jnp.stack(st_s["b_conv"]), jnp.stack(st_s["b_h"]),
            jnp.stack(st_s["c_k"]), jnp.stack(st_s["c_v"]), jnp.stack(st_s["c_logf"]))
```
